```python
import jax, jax.numpy as jnp
from jax import lax
import numpy as np

D_MODEL = 1024
BATCH = 8
SEQ = 4096
DEPTH = 2

N_MIXERS = 2
N_ATTN_LAYERS = (DEPTH + 1) // 2
N_SGU_LAYERS = DEPTH // 2

HEAD_DIM = 64
N_Q_HEADS = D_MODEL // HEAD_DIM
N_KV_HEADS = N_Q_HEADS // 4
GQA_GROUP = N_Q_HEADS // N_KV_HEADS
WINDOW = 128
Q_BLOCK = WINDOW
ROPE_THETA = 10000.0
Q_WIDTH = N_Q_HEADS * HEAD_DIM
KV_WIDTH = N_KV_HEADS * HEAD_DIM
QKV_WIDTH = Q_WIDTH + 2 * KV_WIDTH

SGU_WIDTH = D_MODEL
SGU_GROUPS = 8
SGU_GROUP_DIM = SGU_WIDTH // SGU_GROUPS
SGU_CHUNK = 128

D_FF = ((8 * D_MODEL + 3 * 256 - 1) // (3 * 256)) * 256

EPS = 1e-6

kernel_name = "hybrid_swa_sink_gqa_chunked_sgu_swiglu"


def rmsnorm(x, g):
    xf = x.astype(jnp.float32)
    y = xf * lax.rsqrt(jnp.mean(xf * xf, axis=-1, keepdims=True) + EPS)
    return (y * g.astype(jnp.float32)).astype(x.dtype)


def layernorm(x, g, b):
    xf = x.astype(jnp.float32)
    mu = jnp.mean(xf, axis=-1, keepdims=True)
    var = jnp.mean(jnp.square(xf - mu), axis=-1, keepdims=True)
    y = (xf - mu) * lax.rsqrt(var + EPS)
    return (y * g.astype(jnp.float32) + b.astype(jnp.float32)).astype(x.dtype)


def rope(x, pos):
    half = HEAD_DIM // 2
    inv_freq = ROPE_THETA ** (-(jnp.arange(half, dtype=jnp.float32) * 2.0) / HEAD_DIM)
    ang = pos.astype(jnp.float32)[:, None] * inv_freq[None, :]
    cos = jnp.cos(ang)[None, :, None, :].astype(x.dtype)
    sin = jnp.sin(ang)[None, :, None, :].astype(x.dtype)
    x1, x2 = x[..., :half], x[..., half:]
    return jnp.concatenate([x1 * cos - x2 * sin, x2 * cos + x1 * sin], axis=-1)


def swa_sink_attention(h, w_qkv, b_qkv, sinks, w_o, b_o):
    B, S, _ = h.shape
    nb = S // Q_BLOCK
    T = Q_BLOCK
    qkv = h @ w_qkv + b_qkv
    q = qkv[..., :Q_WIDTH].reshape(B, S, N_Q_HEADS, HEAD_DIM)
    k = qkv[..., Q_WIDTH:Q_WIDTH + KV_WIDTH].reshape(B, S, N_KV_HEADS, HEAD_DIM)
    v = qkv[..., Q_WIDTH + KV_WIDTH:].reshape(B, S, N_KV_HEADS, HEAD_DIM)
    pos = jnp.arange(S, dtype=jnp.int32)
    q = rope(q, pos)
    k = rope(k, pos)
    qb = q.reshape(B, nb, T, N_KV_HEADS, GQA_GROUP, HEAD_DIM)
    kb = k.reshape(B, nb, T, N_KV_HEADS, HEAD_DIM)
    vb = v.reshape(B, nb, T, N_KV_HEADS, HEAD_DIM)
    pad = jnp.zeros_like(kb[:, :1])
    kk = jnp.concatenate([jnp.concatenate([pad, kb[:, :-1]], axis=1), kb], axis=2)
    vv = jnp.concatenate([jnp.concatenate([pad, vb[:, :-1]], axis=1), vb], axis=2)
    scale = HEAD_DIM ** -0.5
    s = jnp.einsum('bnqhgd,bnkhd->bnhgqk', qb, kk).astype(jnp.float32) * scale
    qpos = jnp.arange(T)[:, None] + T
    kpos = jnp.arange(2 * T)[None, :]
    band = (kpos <= qpos) & (qpos - kpos < WINDOW)
    blk = jnp.arange(nb)[:, None, None]
    valid = band[None] & ((blk * T + kpos[None] - T) >= 0)
    s = jnp.where(valid[None, :, None, None], s, -jnp.inf)
    sink = sinks.astype(jnp.float32).reshape(1, 1, N_KV_HEADS, GQA_GROUP, 1, 1)
    m = jnp.maximum(jnp.max(s, axis=-1, keepdims=True), sink)
    p = jnp.exp(s - m)
    denom = jnp.sum(p, axis=-1, keepdims=True) + jnp.exp(sink - m)
    probs = (p / denom).astype(vv.dtype)
    o = jnp.einsum('bnhgqk,bnkhd->bnqhgd', probs, vv).reshape(B, S, Q_WIDTH)
    return o @ w_o + b_o


def chunked_sgu(h, w_in, ln_g, ln_b, w_spatial, b_spatial, w_out):
    B, S, _ = h.shape
    nc = S // SGU_CHUNK
    z = jax.nn.gelu(h @ w_in)
    u, v = z[..., :SGU_WIDTH], z[..., SGU_WIDTH:]
    v = layernorm(v, ln_g, ln_b)
    vg = v.reshape(B, nc, SGU_CHUNK, SGU_GROUPS, SGU_GROUP_DIM)
    causal = jnp.tril(jnp.ones((SGU_CHUNK, SGU_CHUNK), dtype=w_spatial.dtype))
    ws = w_spatial * causal[None]
    mixed = jnp.einsum('gts,bnsgd->bntgd', ws, vg) + b_spatial.T[None, None, :, :, None]
    y = u * mixed.reshape(B, S, SGU_WIDTH)
    return y @ w_out


def swiglu(h, w_gate_up, w_down):
    gu = h @ w_gate_up
    return (jax.nn.silu(gu[..., :D_FF]) * gu[..., D_FF:]) @ w_down


def setup_inputs(seed: int = 0) -> dict:
    key = jax.random.key(seed)
    ks = jax.random.split(key, 20)
    f32 = jnp.float32

    def nrm(k, shape, scale):
        return jax.random.normal(k, shape, f32) * scale

    def gain(k, shape):
        return 1.0 + 0.02 * jax.random.normal(k, shape, f32)

    NA, NS = N_ATTN_LAYERS, N_SGU_LAYERS
    return {
        "x": jax.random.normal(ks[0], (BATCH, SEQ, D_MODEL), f32),
        "norm_mix_pre": gain(ks[1], (DEPTH, D_MODEL)),
        "norm_mix_post": gain(ks[2], (DEPTH, D_MODEL)),
        "norm_ffn_pre": gain(ks[3], (DEPTH, D_MODEL)),
        "norm_ffn_post": gain(ks[4], (DEPTH, D_MODEL)),
        "attn_w_qkv": nrm(ks[5], (NA, D_MODEL, QKV_WIDTH), D_MODEL ** -0.5),
        "attn_b_qkv": nrm(ks[6], (NA, QKV_WIDTH), 0.02),
        "attn_sinks": nrm(ks[7], (NA, N_Q_HEADS), 1.0),
        "attn_w_o": nrm(ks[8], (NA, Q_WIDTH, D_MODEL), Q_WIDTH ** -0.5),
        "attn_b_o": nrm(ks[9], (NA, D_MODEL), 0.02),
        "sgu_w_in": nrm(ks[10], (NS, D_MODEL, 2 * SGU_WIDTH), D_MODEL ** -0.5),
        "sgu_ln_g": gain(ks[11], (NS, SGU_WIDTH)),
        "sgu_ln_b": nrm(ks[12], (NS, SGU_WIDTH), 0.02),
        "sgu_w_spatial": nrm(ks[13], (NS, SGU_GROUPS, SGU_CHUNK, SGU_CHUNK), SGU_CHUNK ** -0.5),
        "sgu_b_spatial": 1.0 + nrm(ks[14], (NS, SGU_GROUPS, SGU_CHUNK), 0.02),
        "sgu_w_out": nrm(ks[15], (NS, SGU_WIDTH, D_MODEL), SGU_WIDTH ** -0.5),
        "ffn_w_gate_up": nrm(ks[16], (DEPTH, D_MODEL, 2 * D_FF), D_MODEL ** -0.5),
        "ffn_w_down": nrm(ks[17], (DEPTH, D_FF, D_MODEL), D_FF ** -0.5),
    }


def reference(x, norm_mix_pre, norm_mix_post, norm_ffn_pre, norm_ffn_post,
              attn_w_qkv, attn_b_qkv, attn_sinks, attn_w_o, attn_b_o,
              sgu_w_in, sgu_ln_g, sgu_ln_b, sgu_w_spatial, sgu_b_spatial, sgu_w_out,
              ffn_w_gate_up, ffn_w_down):
    for i in range(DEPTH):
        h = rmsnorm(x, norm_mix_pre[i])
        j = i // N_MIXERS
        if i % N_MIXERS == 0:
            m = swa_sink_attention(h, attn_w_qkv[j], attn_b_qkv[j], attn_sinks[j],
                                   attn_w_o[j], attn_b_o[j])
        else:
            m = chunked_sgu(h, sgu_w_in[j], sgu_ln_g[j], sgu_ln_b[j],
                            sgu_w_spatial[j], sgu_b_spatial[j], sgu_w_out[j])
        x = x + rmsnorm(m, norm_mix_post[i])
        h = rmsnorm(x, norm_ffn_pre[i])
        x = x + rmsnorm(swiglu(h, ffn_w_gate_up[i], ffn_w_down[i]), norm_ffn_post[i])
    return x
```

```python
import functools

import jax
import jax.numpy as jnp
from jax import lax
from jax.experimental import pallas as pl
from jax.experimental.pallas import tpu as pltpu

F32 = jnp.float32
BF16 = jnp.bfloat16

D_MODEL = 1024
HEAD_DIM = 64
N_Q_HEADS = 16
N_KV_HEADS = 4
GQA_GROUP = 4
WINDOW = 128
Q_WIDTH = N_Q_HEADS * HEAD_DIM
KV_WIDTH = N_KV_HEADS * HEAD_DIM
QKV_WIDTH = Q_WIDTH + 2 * KV_WIDTH
ROPE_THETA = 10000.0
SGU_GROUPS = 8
D_FF = 2816
EPS = 1e-6
LANES = 128

ATTN_TILE = 512
SGU_TILE = 512
FFN_TILE = 512
VMEM_LIMIT = 56 * 1024 * 1024


def _rms(x, g):
    return x * lax.rsqrt(jnp.mean(x * x, axis=-1, keepdims=True) + EPS) * g


def _dot(a, b):
    return jnp.dot(a, b, preferred_element_type=F32)


def _const_spec(shape):
    return pl.BlockSpec(shape, lambda *_: (0,) * len(shape), pipeline_mode=pl.Buffered(1))


def _ffn_kernel(x_ref, gpre_ref, gpost_ref, wgu_ref, wd_ref, o_ref):
    x = x_ref[...]
    h = _rms(x, gpre_ref[...]).astype(BF16)
    gu = _dot(h, wgu_ref[...])
    g = gu[:, :D_FF]
    u = gu[:, D_FF:]
    a = (g * jax.nn.sigmoid(g)) * u
    y = _dot(a.astype(BF16), wd_ref[...])
    o_ref[...] = x + _rms(y, gpost_ref[...])


def _ffn(x2, gpre, gpost, wgu, wd):
    m = x2.shape[0]
    tile = pl.BlockSpec((FFN_TILE, D_MODEL), lambda i: (i, 0))
    return pl.pallas_call(
        _ffn_kernel,
        grid=(m // FFN_TILE,),
        in_specs=[tile, _const_spec((1, D_MODEL)), _const_spec((1, D_MODEL)),
                  _const_spec((D_MODEL, 2 * D_FF)), _const_spec((D_FF, D_MODEL))],
        out_specs=tile,
        out_shape=jax.ShapeDtypeStruct(x2.shape, F32),
        compiler_params=pltpu.CompilerParams(
            dimension_semantics=("arbitrary",), vmem_limit_bytes=VMEM_LIMIT),
        name="ffn",
    )(x2, gpre, gpost, wgu, wd)


def _gelu_tanh(x):
    c = 0.7978845608028654
    return x * (0.5 * (1.0 + jnp.tanh(c * (x + 0.044715 * (x * x * x)))))


def _sgu_kernel(x_ref, gpre_ref, gpost_ref, win_ref, lng_ref, lnb_ref, ws_ref, bsp_ref,
                wout_ref, o_ref):
    x = x_ref[...]
    h = _rms(x, gpre_ref[...]).astype(BF16)
    z = _gelu_tanh(_dot(h, win_ref[...]))
    u = z[:, :D_MODEL]
    v = z[:, D_MODEL:]
    mu = jnp.mean(v, axis=-1, keepdims=True)
    vc = v - mu
    var = jnp.mean(vc * vc, axis=-1, keepdims=True)
    vn = (vc * lax.rsqrt(var + EPS) * lng_ref[...] + lnb_ref[...]).astype(BF16)

    row = lax.broadcasted_iota(jnp.int32, (WINDOW, WINDOW), 0)
    col = lax.broadcasted_iota(jnp.int32, (WINDOW, WINDOW), 1)
    causal = col <= row
    ws = [jnp.where(causal, ws_ref[g], jnp.zeros((), BF16)) for g in range(SGU_GROUPS)]
    bias = bsp_ref[...]
    chunks = []
    for c in range(SGU_TILE // WINDOW):
        r0 = c * WINDOW
        cols = [_dot(ws[g], vn[r0:r0 + WINDOW, g * LANES:(g + 1) * LANES])
                for g in range(SGU_GROUPS)]
        chunks.append(jnp.concatenate(cols, axis=1) + bias)
    mixed = jnp.concatenate(chunks, axis=0)
    y = _dot((u * mixed).astype(BF16), wout_ref[...])
    o_ref[...] = x + _rms(y, gpost_ref[...])


def _sgu(x2, gpre, gpost, win, lng, lnb, ws, bsp, wout):
    m = x2.shape[0]
    tile = pl.BlockSpec((SGU_TILE, D_MODEL), lambda i: (i, 0))
    return pl.pallas_call(
        _sgu_kernel,
        grid=(m // SGU_TILE,),
        in_specs=[tile, _const_spec((1, D_MODEL)), _const_spec((1, D_MODEL)),
                  _const_spec((D_MODEL, 2 * D_MODEL)), _const_spec((1, D_MODEL)),
                  _const_spec((1, D_MODEL)), _const_spec((SGU_GROUPS, WINDOW, WINDOW)),
                  _const_spec((WINDOW, D_MODEL)), _const_spec((D_MODEL, D_MODEL))],
        out_specs=tile,
        out_shape=jax.ShapeDtypeStruct(x2.shape, F32),
        compiler_params=pltpu.CompilerParams(
            dimension_semantics=("arbitrary",), vmem_limit_bytes=VMEM_LIMIT),
        name="sgu",
    )(x2, gpre, gpost, win, lng, lnb, ws, bsp, wout)


def _rope(t, cos, sin_signed, lo_half):
    outs = []
    for c in range(t.shape[1] // LANES):
        tc = t[:, c * LANES:(c + 1) * LANES]
        partner = jnp.where(lo_half, pltpu.roll(tc, LANES - HEAD_DIM // 2, axis=1),
                            pltpu.roll(tc, HEAD_DIM // 2, axis=1))
        outs.append(tc * cos + partner * sin_signed)
    return jnp.concatenate(outs, axis=1)


def _attn_kernel(sink_ref, x_ref, gpre_ref, gpost_ref, wqkv_ref, bqkv_ref, cos_ref, sin_ref,
                 wo_ref, bo_ref, o_ref, kt_prev_ref, vd_prev_ref):
    j = pl.program_id(1)

    @pl.when(j == 0)
    def _():
        kt_prev_ref[...] = jnp.zeros_like(kt_prev_ref)
        vd_prev_ref[...] = jnp.zeros_like(vd_prev_ref)

    x = x_ref[0]
    h = _rms(x, gpre_ref[...]).astype(BF16)
    qkv = _dot(h, wqkv_ref[...]) + bqkv_ref[...]

    lane = lax.broadcasted_iota(jnp.int32, (ATTN_TILE, LANES), 1)
    lo_half = (lane % HEAD_DIM) < (HEAD_DIM // 2)
    cos = cos_ref[...]
    sin_signed = sin_ref[...]
    q = (_rope(qkv[:, :Q_WIDTH], cos, sin_signed, lo_half) * (HEAD_DIM ** -0.5)).astype(BF16)
    k = _rope(qkv[:, Q_WIDTH:Q_WIDTH + KV_WIDTH], cos, sin_signed, lo_half)
    v = qkv[:, Q_WIDTH + KV_WIDTH:]

    lane_lo = lane < HEAD_DIM
    vd_cols = []
    for c in range(KV_WIDTH // LANES):
        vc = v[:, c * LANES:(c + 1) * LANES]
        vr = pltpu.roll(vc, HEAD_DIM, axis=1)
        vd_cols.append(jnp.where(lane_lo, vc, vr))
        vd_cols.append(jnp.where(lane_lo, vr, vc))
    vd = jnp.concatenate(vd_cols, axis=1).astype(BF16)

    m4 = GQA_GROUP * WINDOW
    row4 = lax.broadcasted_iota(jnp.int32, (m4, WINDOW), 0)
    col4 = lax.broadcasted_iota(jnp.int32, (m4, WINDOW), 1)
    causal = col4 <= (row4 % WINDOW)
    head_in_group = lax.broadcasted_iota(jnp.int32, (m4, 1), 0) // WINDOW
    lane_q = lax.broadcasted_iota(jnp.int32, (WINDOW, LANES), 1)
    q_lo = lane_q < HEAD_DIM
    zero_bf = jnp.zeros((), BF16)
    neg_inf = jnp.where(j == 0, -jnp.inf, 0.0).astype(F32)

    kt_prev = kt_prev_ref[...]
    vd_prev = vd_prev_ref[...]
    o_blocks = []
    for i in range(ATTN_TILE // WINDOW):
        r0 = i * WINDOW
        kt_cur = k[r0:r0 + WINDOW, :].T.astype(BF16)
        vd_cur = vd[r0:r0 + WINDOW, :]
        o_cols = []
        for g in range(N_KV_HEADS):
            ktg = jnp.concatenate([kt_prev[g * HEAD_DIM:(g + 1) * HEAD_DIM, :],
                                   kt_cur[g * HEAD_DIM:(g + 1) * HEAD_DIM, :]], axis=1)
            rhs = jnp.concatenate([ktg, ktg], axis=0)
            lhs_parts = []
            for p in range(2):
                qp = q[r0:r0 + WINDOW, (2 * g + p) * LANES:(2 * g + p + 1) * LANES]
                lhs_parts.append(jnp.where(q_lo, qp, zero_bf))
                lhs_parts.append(jnp.where(q_lo, zero_bf, qp))
            s = _dot(jnp.concatenate(lhs_parts, axis=0), rhs)
            s_prev = s[:, :WINDOW]
            if i == 0:
                s_prev = s_prev + neg_inf
            s_sel = jnp.where(causal, s[:, WINDOW:], s_prev)
            sink = jnp.zeros((m4, 1), F32)
            for hh in range(GQA_GROUP):
                sink = jnp.where(head_in_group == hh, sink_ref[g * GQA_GROUP + hh], sink)
            m = jnp.maximum(jnp.max(s_sel, axis=-1, keepdims=True), sink)
            p_all = jnp.exp(s_sel - m)
            denom = jnp.sum(p_all, axis=-1, keepdims=True) + jnp.exp(sink - m)
            p_prev = jnp.where(causal, 0.0, p_all)
            p_cur = jnp.where(causal, p_all, 0.0)
            probs = jnp.concatenate([p_prev, p_cur], axis=1).astype(BF16)
            vg = jnp.concatenate([vd_prev[:, g * LANES:(g + 1) * LANES],
                                  vd_cur[:, g * LANES:(g + 1) * LANES]], axis=0)
            r = _dot(probs, vg) * (1.0 / denom)
            for p in range(2):
                o_cols.append(jnp.where(q_lo, r[(2 * p) * WINDOW:(2 * p + 1) * WINDOW],
                                        r[(2 * p + 1) * WINDOW:(2 * p + 2) * WINDOW]))
        o_blocks.append(jnp.concatenate(o_cols, axis=1).astype(BF16))
        kt_prev = kt_cur
        vd_prev = vd_cur
    kt_prev_ref[...] = kt_prev
    vd_prev_ref[...] = vd_prev

    o = jnp.concatenate(o_blocks, axis=0)
    y = _dot(o, wo_ref[...]) + bo_ref[...]
    o_ref[0] = x + _rms(y, gpost_ref[...])


def _attn(x, gpre, gpost, wqkv, bqkv, sinks, cos, sin_signed, wo, bo):
    b, s, _ = x.shape
    tile = pl.BlockSpec((1, ATTN_TILE, D_MODEL), lambda bi, ji: (bi, ji, 0))
    rope_spec = pl.BlockSpec((ATTN_TILE, LANES), lambda bi, ji: (ji, 0))
    return pl.pallas_call(
        _attn_kernel,
        grid=(b, s // ATTN_TILE),
        in_specs=[pl.BlockSpec(memory_space=pltpu.SMEM),
                  tile, _const_spec((1, D_MODEL)), _const_spec((1, D_MODEL)),
                  _const_spec((D_MODEL, QKV_WIDTH)), _const_spec((1, QKV_WIDTH)),
                  rope_spec, rope_spec,
                  _const_spec((Q_WIDTH, D_MODEL)), _const_spec((1, D_MODEL))],
        out_specs=tile,
        out_shape=jax.ShapeDtypeStruct(x.shape, F32),
        scratch_shapes=[pltpu.VMEM((KV_WIDTH, WINDOW), BF16),
                        pltpu.VMEM((WINDOW, N_KV_HEADS * LANES), BF16)],
        compiler_params=pltpu.CompilerParams(
            dimension_semantics=("arbitrary", "arbitrary"), vmem_limit_bytes=VMEM_LIMIT),
        name="attn",
    )(sinks, x, gpre, gpost, wqkv, bqkv, cos, sin_signed, wo, bo)


def _rope_tables(seq):
    half = HEAD_DIM // 2
    inv_freq = ROPE_THETA ** (-(jnp.arange(half, dtype=F32) * 2.0) / HEAD_DIM)
    ang = jnp.arange(seq, dtype=jnp.int32).astype(F32)[:, None] * inv_freq[None, :]
    cos = jnp.tile(jnp.cos(ang), (1, LANES // half))
    sin = jnp.tile(jnp.sin(ang), (1, LANES // half))
    lane = jnp.arange(LANES)
    sign = jnp.where((lane % HEAD_DIM) < half, -1.0, 1.0).astype(F32)
    return cos, sin * sign[None, :]


def kernel(x, norm_mix_pre, norm_mix_post, norm_ffn_pre, norm_ffn_post, attn_w_qkv, attn_b_qkv,
           attn_sinks, attn_w_o, attn_b_o, sgu_w_in, sgu_ln_g, sgu_ln_b, sgu_w_spatial,
           sgu_b_spatial, sgu_w_out, ffn_w_gate_up, ffn_w_down):
    b, s, d = x.shape
    row = lambda a: a.reshape(1, -1).astype(F32)
    cos, sin_signed = _rope_tables(s)

    x = _attn(x, row(norm_mix_pre[0]), row(norm_mix_post[0]), attn_w_qkv[0].astype(BF16),
              row(attn_b_qkv[0]), attn_sinks[0].astype(F32), cos, sin_signed,
              attn_w_o[0].astype(BF16), row(attn_b_o[0]))
    x2 = x.reshape(b * s, d)
    x2 = _ffn(x2, row(norm_ffn_pre[0]), row(norm_ffn_post[0]),
              ffn_w_gate_up[0].astype(BF16), ffn_w_down[0].astype(BF16))
    bsp = jnp.repeat(sgu_b_spatial[0].T.astype(F32), D_MODEL // SGU_GROUPS, axis=1)
    x2 = _sgu(x2, row(norm_mix_pre[1]), row(norm_mix_post[1]), sgu_w_in[0].astype(BF16),
              row(sgu_ln_g[0]), row(sgu_ln_b[0]), sgu_w_spatial[0].astype(BF16), bsp,
              sgu_w_out[0].astype(BF16))
    x2 = _ffn(x2, row(norm_ffn_pre[1]), row(norm_ffn_post[1]),
              ffn_w_gate_up[1].astype(BF16), ffn_w_down[1].astype(BF16))
    return x2.reshape(b, s, d)
```

```python
import jax
import jax.numpy as jnp
from jax import lax
from jax.experimental import pallas as pl
from jax.experimental.pallas import tpu as pltpu

F32 = jnp.float32
BF16 = jnp.bfloat16

D_MODEL = 1024
HEAD_DIM = 64
HALF = HEAD_DIM // 2
N_Q_HEADS = 16
N_KV_HEADS = 4
GQA_GROUP = 4
WINDOW = 128
Q_WIDTH = N_Q_HEADS * HEAD_DIM
KV_WIDTH = N_KV_HEADS * HEAD_DIM
QKV_WIDTH = Q_WIDTH + 2 * KV_WIDTH
ROPE_THETA = 10000.0
SGU_GROUPS = 8
D_FF = 2816
EPS = 1e-6
LANES = 128

ATTN_TILE = 512
SGU_TILE = 512
FFN_TILE = 512
VMEM_LIMIT = 56 * 1024 * 1024


def _rms(x, g):
    return x * lax.rsqrt(jnp.mean(x * x, axis=-1, keepdims=True) + EPS) * g


def _dot(a, b):
    return jnp.dot(a, b, preferred_element_type=F32)


def _const_spec(shape):
    return pl.BlockSpec(shape, lambda *_: (0,) * len(shape), pipeline_mode=pl.Buffered(1))


def _ffn_kernel(x_ref, gpre_ref, gpost_ref, wgu_ref, wd_ref, o_ref):
    x = x_ref[...]
    h = _rms(x, gpre_ref[...]).astype(BF16)
    gu = _dot(h, wgu_ref[...])
    g = gu[:, :D_FF]
    u = gu[:, D_FF:]
    a = (g * jax.nn.sigmoid(g)) * u
    y = _dot(a.astype(BF16), wd_ref[...])
    o_ref[...] = x + _rms(y, gpost_ref[...])


def _ffn(x2, gpre, gpost, wgu, wd):
    m = x2.shape[0]
    tile = pl.BlockSpec((FFN_TILE, D_MODEL), lambda i: (i, 0))
    return pl.pallas_call(
        _ffn_kernel,
        grid=(m // FFN_TILE,),
        in_specs=[tile, _const_spec((1, D_MODEL)), _const_spec((1, D_MODEL)),
                  _const_spec((D_MODEL, 2 * D_FF)), _const_spec((D_FF, D_MODEL))],
        out_specs=tile,
        out_shape=jax.ShapeDtypeStruct(x2.shape, F32),
        compiler_params=pltpu.CompilerParams(
            dimension_semantics=("arbitrary",), vmem_limit_bytes=VMEM_LIMIT),
        name="ffn",
    )(x2, gpre, gpost, wgu, wd)


def _gelu_tanh(x):
    c = 0.7978845608028654
    return x * (0.5 * (1.0 + jnp.tanh(c * (x + 0.044715 * (x * x * x)))))


def _sgu_kernel(x_ref, gpre_ref, gpost_ref, win_ref, lng_ref, lnb_ref, ws_ref, bsp_ref,
                wout_ref, o_ref):
    x = x_ref[...]
    h = _rms(x, gpre_ref[...]).astype(BF16)
    z = _gelu_tanh(_dot(h, win_ref[...]))
    u = z[:, :D_MODEL]
    v = z[:, D_MODEL:]
    mu = jnp.mean(v, axis=-1, keepdims=True)
    vc = v - mu
    var = jnp.mean(vc * vc, axis=-1, keepdims=True)
    vn = (vc * lax.rsqrt(var + EPS) * lng_ref[...] + lnb_ref[...]).astype(BF16)

    row = lax.broadcasted_iota(jnp.int32, (WINDOW, WINDOW), 0)
    col = lax.broadcasted_iota(jnp.int32, (WINDOW, WINDOW), 1)
    causal = col <= row
    ws = [jnp.where(causal, ws_ref[g], jnp.zeros((), BF16)) for g in range(SGU_GROUPS)]
    bias = bsp_ref[...]
    chunks = []
    for c in range(SGU_TILE // WINDOW):
        r0 = c * WINDOW
        cols = [_dot(ws[g], vn[r0:r0 + WINDOW, g * LANES:(g + 1) * LANES])
                for g in range(SGU_GROUPS)]
        chunks.append(jnp.concatenate(cols, axis=1) + bias)
    mixed = jnp.concatenate(chunks, axis=0)
    y = _dot((u * mixed).astype(BF16), wout_ref[...])
    o_ref[...] = x + _rms(y, gpost_ref[...])


def _sgu(x2, gpre, gpost, win, lng, lnb, ws, bsp, wout):
    m = x2.shape[0]
    tile = pl.BlockSpec((SGU_TILE, D_MODEL), lambda i: (i, 0))
    return pl.pallas_call(
        _sgu_kernel,
        grid=(m // SGU_TILE,),
        in_specs=[tile, _const_spec((1, D_MODEL)), _const_spec((1, D_MODEL)),
                  _const_spec((D_MODEL, 2 * D_MODEL)), _const_spec((1, D_MODEL)),
                  _const_spec((1, D_MODEL)), _const_spec((SGU_GROUPS, WINDOW, WINDOW)),
                  _const_spec((WINDOW, D_MODEL)), _const_spec((D_MODEL, D_MODEL))],
        out_specs=tile,
        out_shape=jax.ShapeDtypeStruct(x2.shape, F32),
        compiler_params=pltpu.CompilerParams(
            dimension_semantics=("arbitrary",), vmem_limit_bytes=VMEM_LIMIT),
        name="sgu",
    )(x2, gpre, gpost, win, lng, lnb, ws, bsp, wout)


def _rope_t(t, cos, sin):
    out = []
    for hd in range(t.shape[0] // HEAD_DIM):
        x1 = t[hd * HEAD_DIM:hd * HEAD_DIM + HALF]
        x2 = t[hd * HEAD_DIM + HALF:(hd + 1) * HEAD_DIM]
        out.append(x1 * cos - x2 * sin)
        out.append(x2 * cos + x1 * sin)
    return jnp.concatenate(out, axis=0)


def _attn_kernel(sink_ref, x_ref, gpre_ref, gpost_ref, wqkv_t_ref, bqkv_t_ref, cos_ref, sin_ref,
                 wo_ref, bo_ref, o_ref, k_prev_ref, vt_prev_ref):
    j = pl.program_id(1)

    @pl.when(j == 0)
    def _():
        k_prev_ref[...] = jnp.zeros_like(k_prev_ref)
        vt_prev_ref[...] = jnp.zeros_like(vt_prev_ref)

    x = x_ref[0]
    h_t = _rms(x, gpre_ref[...]).astype(BF16).T
    qkv_t = _dot(wqkv_t_ref[...], h_t) + bqkv_t_ref[...]

    cos = cos_ref[...]
    sin = sin_ref[...]
    q_t = (_rope_t(qkv_t[:Q_WIDTH], cos, sin) * (HEAD_DIM ** -0.5)).astype(BF16)
    k_t = _rope_t(qkv_t[Q_WIDTH:Q_WIDTH + KV_WIDTH], cos, sin)
    v_t = qkv_t[Q_WIDTH + KV_WIDTH:].astype(BF16)

    n4 = GQA_GROUP * WINDOW
    key_idx = lax.broadcasted_iota(jnp.int32, (WINDOW, n4), 0)
    qry_idx = lax.broadcasted_iota(jnp.int32, (WINDOW, n4), 1) % WINDOW
    causal = key_idx <= qry_idx
    head_in_group = lax.broadcasted_iota(jnp.int32, (1, n4), 1) // WINDOW
    zero_rows = jnp.zeros((HEAD_DIM, n4), BF16)
    neg_inf = jnp.where(j == 0, -jnp.inf, 0.0).astype(F32)

    n_blk = ATTN_TILE // WINDOW
    k_blk = [k_prev_ref[...]]
    vt_blk = [vt_prev_ref[...]]
    for i in range(n_blk):
        t0 = i * WINDOW
        k_blk.append(k_t[:, t0:t0 + WINDOW].T.astype(BF16))
        vt_blk.append(v_t[:, t0:t0 + WINDOW])
    k_prev_ref[...] = k_blk[n_blk]
    vt_prev_ref[...] = vt_blk[n_blk]

    def scores(i, g):
        t0 = i * WINDOW
        c = g // 2
        keys = jnp.concatenate([k_blk[i][:, c * LANES:(c + 1) * LANES],
                                k_blk[i + 1][:, c * LANES:(c + 1) * LANES]], axis=0)
        q_row = jnp.concatenate(
            [q_t[(g * GQA_GROUP + hh) * HEAD_DIM:(g * GQA_GROUP + hh + 1) * HEAD_DIM, t0:t0 + WINDOW]
             for hh in range(GQA_GROUP)], axis=1)
        rhs = jnp.concatenate([q_row, zero_rows] if g % 2 == 0 else [zero_rows, q_row], axis=0)
        return _dot(keys, rhs)

    def attend(i, g, s_t):
        s_prev = s_t[:WINDOW]
        if i == 0:
            s_prev = s_prev + neg_inf
        s_sel = jnp.where(causal, s_t[WINDOW:], s_prev)
        sink = jnp.zeros((1, n4), F32)
        for hh in range(GQA_GROUP):
            sink = jnp.where(head_in_group == hh, sink_ref[g * GQA_GROUP + hh], sink)
        m = jnp.maximum(jnp.max(s_sel, axis=0, keepdims=True), sink)
        p_all = jnp.exp(s_sel - m)
        denom = jnp.sum(p_all, axis=0, keepdims=True) + jnp.exp(sink - m)
        probs = jnp.concatenate([jnp.where(causal, 0.0, p_all),
                                 jnp.where(causal, p_all, 0.0)], axis=0).astype(BF16)
        vals = jnp.concatenate([vt_blk[i][g * HEAD_DIM:(g + 1) * HEAD_DIM],
                                vt_blk[i + 1][g * HEAD_DIM:(g + 1) * HEAD_DIM]], axis=1)
        return _dot(vals, probs) * (1.0 / denom)

    units = [(i, g) for i in range(n_blk) for g in range(N_KV_HEADS)]
    o_rows = [[] for _ in range(n_blk)]
    s_next = scores(*units[0])
    for u, (i, g) in enumerate(units):
        s_t = s_next
        if u + 1 < len(units):
            s_next = scores(*units[u + 1])
        o_t = attend(i, g, s_t)
        for hh in range(GQA_GROUP):
            o_rows[i].append(o_t[:, hh * WINDOW:(hh + 1) * WINDOW])
    o_t_blocks = [jnp.concatenate(o_rows[i], axis=0).astype(BF16) for i in range(n_blk)]

    o = jnp.concatenate(o_t_blocks, axis=1).T
    y = _dot(o, wo_ref[...]) + bo_ref[...]
    o_ref[0] = x + _rms(y, gpost_ref[...])


def _attn(x, gpre, gpost, wqkv_t, bqkv_t, sinks, cos_t, sin_t, wo, bo):
    b, s, _ = x.shape
    tile = pl.BlockSpec((1, ATTN_TILE, D_MODEL), lambda bi, ji: (bi, ji, 0))
    rope_spec = pl.BlockSpec((HALF, ATTN_TILE), lambda bi, ji: (0, ji))
    return pl.pallas_call(
        _attn_kernel,
        grid=(b, s // ATTN_TILE),
        in_specs=[pl.BlockSpec(memory_space=pltpu.SMEM),
                  tile, _const_spec((1, D_MODEL)), _const_spec((1, D_MODEL)),
                  _const_spec((QKV_WIDTH, D_MODEL)), _const_spec((QKV_WIDTH, 1)),
                  rope_spec, rope_spec,
                  _const_spec((Q_WIDTH, D_MODEL)), _const_spec((1, D_MODEL))],
        out_specs=tile,
        out_shape=jax.ShapeDtypeStruct(x.shape, F32),
        scratch_shapes=[pltpu.VMEM((WINDOW, KV_WIDTH), BF16),
                        pltpu.VMEM((KV_WIDTH, WINDOW), BF16)],
        compiler_params=pltpu.CompilerParams(
            dimension_semantics=("arbitrary", "arbitrary"), vmem_limit_bytes=VMEM_LIMIT),
        name="attn",
    )(sinks, x, gpre, gpost, wqkv_t, bqkv_t, cos_t, sin_t, wo, bo)


def _rope_tables_t(seq):
    inv_freq = ROPE_THETA ** (-(jnp.arange(HALF, dtype=F32) * 2.0) / HEAD_DIM)
    ang = inv_freq[:, None] * jnp.arange(seq, dtype=jnp.int32).astype(F32)[None, :]
    return jnp.cos(ang), jnp.sin(ang)


def kernel(x, norm_mix_pre, norm_mix_post, norm_ffn_pre, norm_ffn_post, attn_w_qkv, attn_b_qkv,
           attn_sinks, attn_w_o, attn_b_o, sgu_w_in, sgu_ln_g, sgu_ln_b, sgu_w_spatial,
           sgu_b_spatial, sgu_w_out, ffn_w_gate_up, ffn_w_down):
    b, s, d = x.shape
    row = lambda a: a.reshape(1, -1).astype(F32)
    cos_t, sin_t = _rope_tables_t(s)

    x = _attn(x, row(norm_mix_pre[0]), row(norm_mix_post[0]), attn_w_qkv[0].T.astype(BF16),
              attn_b_qkv[0].reshape(-1, 1).astype(F32), attn_sinks[0].astype(F32), cos_t, sin_t,
              attn_w_o[0].astype(BF16), row(attn_b_o[0]))
    x2 = x.reshape(b * s, d)
    x2 = _ffn(x2, row(norm_ffn_pre[0]), row(norm_ffn_post[0]),
              ffn_w_gate_up[0].astype(BF16), ffn_w_down[0].astype(BF16))
    bsp = jnp.repeat(sgu_b_spatial[0].T.astype(F32), D_MODEL // SGU_GROUPS, axis=1)
    x2 = _sgu(x2, row(norm_mix_pre[1]), row(norm_mix_post[1]), sgu_w_in[0].astype(BF16),
              row(sgu_ln_g[0]), row(sgu_ln_b[0]), sgu_w_spatial[0].astype(BF16), bsp,
              sgu_w_out[0].astype(BF16))
    x2 = _ffn(x2, row(norm_ffn_pre[1]), row(norm_ffn_post[1]),
              ffn_w_gate_up[1].astype(BF16), ffn_w_down[1].astype(BF16))
    return x2.reshape(b, s, d)
```

```python
import jax
import jax.numpy as jnp
from jax import lax
from jax.experimental import pallas as pl
from jax.experimental.pallas import tpu as pltpu

F32 = jnp.float32
BF16 = jnp.bfloat16

D_MODEL = 1024
HEAD_DIM = 64
HALF = HEAD_DIM // 2
N_Q_HEADS = 16
N_KV_HEADS = 4
GQA_GROUP = 4
WINDOW = 128
Q_WIDTH = N_Q_HEADS * HEAD_DIM
KV_WIDTH = N_KV_HEADS * HEAD_DIM
QKV_WIDTH = Q_WIDTH + 2 * KV_WIDTH
ROPE_THETA = 10000.0
SGU_GROUPS = 8
D_FF = 2816
EPS = 1e-6
LANES = 128

ATTN_TILE = 512
SGU_TILE = 1024
FFN_TILE = 1024
SUB_TILE = 256
VMEM_LIMIT = 56 * 1024 * 1024


def _rms(x, g):
    return x * lax.rsqrt(jnp.mean(x * x, axis=-1, keepdims=True) + EPS) * g


def _dot(a, b):
    return jnp.dot(a, b, preferred_element_type=F32)


def _skewed(n_sub, stages):
    carried = [dict() for _ in stages]
    for slot in range(n_sub + len(stages) - 1):
        for k, stage in enumerate(stages):
            t = slot - k
            if 0 <= t < n_sub:
                carried[k][t] = stage(t, carried[k - 1].pop(t) if k else None)


def _const_spec(shape):
    return pl.BlockSpec(shape, lambda *_: (0,) * len(shape), pipeline_mode=pl.Buffered(1))


def _ffn_kernel(x_ref, gpre_ref, gpost_ref, wgu_ref, wd_ref, o_ref):
    gpre = gpre_ref[...]
    gpost = gpost_ref[...]

    def rows(t):
        return pl.ds(t * SUB_TILE, SUB_TILE)

    def norm(t, _):
        return _rms(x_ref[rows(t), :], gpre).astype(BF16)

    def gate_up(t, h):
        return _dot(h, wgu_ref[...])

    def act(t, gu):
        g = gu[:, :D_FF]
        return ((g * jax.nn.sigmoid(g)) * gu[:, D_FF:]).astype(BF16)

    def down(t, a):
        return _dot(a, wd_ref[...])

    def finish(t, y):
        o_ref[rows(t), :] = x_ref[rows(t), :] + _rms(y, gpost)

    _skewed(FFN_TILE // SUB_TILE, [norm, gate_up, act, down, finish])


def _ffn(x2, gpre, gpost, wgu, wd):
    m = x2.shape[0]
    tile = pl.BlockSpec((FFN_TILE, D_MODEL), lambda i: (i, 0))
    return pl.pallas_call(
        _ffn_kernel,
        grid=(m // FFN_TILE,),
        in_specs=[tile, _const_spec((1, D_MODEL)), _const_spec((1, D_MODEL)),
                  _const_spec((D_MODEL, 2 * D_FF)), _const_spec((D_FF, D_MODEL))],
        out_specs=tile,
        out_shape=jax.ShapeDtypeStruct(x2.shape, F32),
        compiler_params=pltpu.CompilerParams(
            dimension_semantics=("arbitrary",), vmem_limit_bytes=VMEM_LIMIT),
        name="ffn",
    )(x2, gpre, gpost, wgu, wd)


def _gelu_tanh(x):
    c = 0.7978845608028654
    return x * (0.5 * (1.0 + jnp.tanh(c * (x + 0.044715 * (x * x * x)))))


def _sgu_kernel(x_ref, gpre_ref, gpost_ref, win_ref, lng_ref, lnb_ref, ws_ref, bsp_ref,
                wout_ref, o_ref):
    gpre = gpre_ref[...]
    gpost = gpost_ref[...]
    row = lax.broadcasted_iota(jnp.int32, (WINDOW, WINDOW), 0)
    col = lax.broadcasted_iota(jnp.int32, (WINDOW, WINDOW), 1)
    causal = col <= row
    ws = [jnp.where(causal, ws_ref[g], jnp.zeros((), BF16)) for g in range(SGU_GROUPS)]
    bias = bsp_ref[...]

    def rows(t):
        return pl.ds(t * SUB_TILE, SUB_TILE)

    def norm(t, _):
        return _rms(x_ref[rows(t), :], gpre).astype(BF16)

    def project(t, h):
        return _dot(h, win_ref[...])

    def gate(t, z):
        z = _gelu_tanh(z)
        u = z[:, :D_MODEL]
        v = z[:, D_MODEL:]
        mu = jnp.mean(v, axis=-1, keepdims=True)
        vc = v - mu
        var = jnp.mean(vc * vc, axis=-1, keepdims=True)
        vn = (vc * lax.rsqrt(var + EPS) * lng_ref[...] + lnb_ref[...]).astype(BF16)
        return u, vn

    def mix(t, uv):
        u, vn = uv
        chunks = []
        for c in range(SUB_TILE // WINDOW):
            r0 = c * WINDOW
            cols = [_dot(ws[g], vn[r0:r0 + WINDOW, g * LANES:(g + 1) * LANES])
                    for g in range(SGU_GROUPS)]
            chunks.append(jnp.concatenate(cols, axis=1) + bias)
        mixed = jnp.concatenate(chunks, axis=0)
        return _dot((u * mixed).astype(BF16), wout_ref[...])

    def finish(t, y):
        o_ref[rows(t), :] = x_ref[rows(t), :] + _rms(y, gpost)

    _skewed(SGU_TILE // SUB_TILE, [norm, project, gate, mix, finish])


def _sgu(x2, gpre, gpost, win, lng, lnb, ws, bsp, wout):
    m = x2.shape[0]
    tile = pl.BlockSpec((SGU_TILE, D_MODEL), lambda i: (i, 0))
    return pl.pallas_call(
        _sgu_kernel,
        grid=(m // SGU_TILE,),
        in_specs=[tile, _const_spec((1, D_MODEL)), _const_spec((1, D_MODEL)),
                  _const_spec((D_MODEL, 2 * D_MODEL)), _const_spec((1, D_MODEL)),
                  _const_spec((1, D_MODEL)), _const_spec((SGU_GROUPS, WINDOW, WINDOW)),
                  _const_spec((WINDOW, D_MODEL)), _const_spec((D_MODEL, D_MODEL))],
        out_specs=tile,
        out_shape=jax.ShapeDtypeStruct(x2.shape, F32),
        compiler_params=pltpu.CompilerParams(
            dimension_semantics=("arbitrary",), vmem_limit_bytes=VMEM_LIMIT),
        name="sgu",
    )(x2, gpre, gpost, win, lng, lnb, ws, bsp, wout)


def _rope_t(t, cos, sin):
    out = []
    for hd in range(t.shape[0] // HEAD_DIM):
        x1 = t[hd * HEAD_DIM:hd * HEAD_DIM + HALF]
        x2 = t[hd * HEAD_DIM + HALF:(hd + 1) * HEAD_DIM]
        out.append(x1 * cos - x2 * sin)
        out.append(x2 * cos + x1 * sin)
    return jnp.concatenate(out, axis=0)


def _attn_kernel(sink_ref, x_ref, gpre_ref, gpost_ref, wqkv_t_ref, bqkv_t_ref, cos_ref, sin_ref,
                 wo_ref, bo_ref, o_ref, k_prev_ref, vt_prev_ref):
    j = pl.program_id(1)

    @pl.when(j == 0)
    def _():
        k_prev_ref[...] = jnp.zeros_like(k_prev_ref)
        vt_prev_ref[...] = jnp.zeros_like(vt_prev_ref)

    x = x_ref[0]
    h_t = _rms(x, gpre_ref[...]).astype(BF16).T
    qkv_t = _dot(wqkv_t_ref[...], h_t) + bqkv_t_ref[...]

    cos = cos_ref[...]
    sin = sin_ref[...]
    q_t = (_rope_t(qkv_t[:Q_WIDTH], cos, sin) * (HEAD_DIM ** -0.5)).astype(BF16)
    k_t = _rope_t(qkv_t[Q_WIDTH:Q_WIDTH + KV_WIDTH], cos, sin)
    v_t = qkv_t[Q_WIDTH + KV_WIDTH:].astype(BF16)

    n4 = GQA_GROUP * WINDOW
    key_idx = lax.broadcasted_iota(jnp.int32, (WINDOW, n4), 0)
    qry_idx = lax.broadcasted_iota(jnp.int32, (WINDOW, n4), 1) % WINDOW
    causal = key_idx <= qry_idx
    head_in_group = lax.broadcasted_iota(jnp.int32, (1, n4), 1) // WINDOW
    zero_rows = jnp.zeros((HEAD_DIM, n4), BF16)
    neg_inf = jnp.where(j == 0, -jnp.inf, 0.0).astype(F32)

    n_blk = ATTN_TILE // WINDOW
    k_blk = [k_prev_ref[...]]
    vt_blk = [vt_prev_ref[...]]
    for i in range(n_blk):
        t0 = i * WINDOW
        k_blk.append(k_t[:, t0:t0 + WINDOW].T.astype(BF16))
        vt_blk.append(v_t[:, t0:t0 + WINDOW])
    k_prev_ref[...] = k_blk[n_blk]
    vt_prev_ref[...] = vt_blk[n_blk]

    def scores(i, g):
        t0 = i * WINDOW
        c = g // 2
        keys = jnp.concatenate([k_blk[i][:, c * LANES:(c + 1) * LANES],
                                k_blk[i + 1][:, c * LANES:(c + 1) * LANES]], axis=0)
        q_row = jnp.concatenate(
            [q_t[(g * GQA_GROUP + hh) * HEAD_DIM:(g * GQA_GROUP + hh + 1) * HEAD_DIM, t0:t0 + WINDOW]
             for hh in range(GQA_GROUP)], axis=1)
        rhs = jnp.concatenate([q_row, zero_rows] if g % 2 == 0 else [zero_rows, q_row], axis=0)
        return _dot(keys, rhs)

    def attend(i, g, s_t):
        s_prev = s_t[:WINDOW]
        if i == 0:
            s_prev = s_prev + neg_inf
        s_sel = jnp.where(causal, s_t[WINDOW:], s_prev)
        sink = jnp.zeros((1, n4), F32)
        for hh in range(GQA_GROUP):
            sink = jnp.where(head_in_group == hh, sink_ref[g * GQA_GROUP + hh], sink)
        m = jnp.maximum(jnp.max(s_sel, axis=0, keepdims=True), sink)
        p_all = jnp.exp(s_sel - m)
        denom = jnp.sum(p_all, axis=0, keepdims=True) + jnp.exp(sink - m)
        probs = jnp.concatenate([jnp.where(causal, 0.0, p_all),
                                 jnp.where(causal, p_all, 0.0)], axis=0).astype(BF16)
        vals = jnp.concatenate([vt_blk[i][g * HEAD_DIM:(g + 1) * HEAD_DIM],
                                vt_blk[i + 1][g * HEAD_DIM:(g + 1) * HEAD_DIM]], axis=1)
        return _dot(vals, probs) * (1.0 / denom)

    units = [(i, g) for i in range(n_blk) for g in range(N_KV_HEADS)]
    o_rows = [[] for _ in range(n_blk)]
    s_next = scores(*units[0])
    for u, (i, g) in enumerate(units):
        s_t = s_next
        if u + 1 < len(units):
            s_next = scores(*units[u + 1])
        o_t = attend(i, g, s_t)
        for hh in range(GQA_GROUP):
            o_rows[i].append(o_t[:, hh * WINDOW:(hh + 1) * WINDOW])
    o_t_blocks = [jnp.concatenate(o_rows[i], axis=0).astype(BF16) for i in range(n_blk)]

    o = jnp.concatenate(o_t_blocks, axis=1).T
    y = _dot(o, wo_ref[...]) + bo_ref[...]
    o_ref[0] = x + _rms(y, gpost_ref[...])


def _attn(x, gpre, gpost, wqkv_t, bqkv_t, sinks, cos_t, sin_t, wo, bo):
    b, s, _ = x.shape
    tile = pl.BlockSpec((1, ATTN_TILE, D_MODEL), lambda bi, ji: (bi, ji, 0))
    rope_spec = pl.BlockSpec((HALF, ATTN_TILE), lambda bi, ji: (0, ji))
    return pl.pallas_call(
        _attn_kernel,
        grid=(b, s // ATTN_TILE),
        in_specs=[pl.BlockSpec(memory_space=pltpu.SMEM),
                  tile, _const_spec((1, D_MODEL)), _const_spec((1, D_MODEL)),
                  _const_spec((QKV_WIDTH, D_MODEL)), _const_spec((QKV_WIDTH, 1)),
                  rope_spec, rope_spec,
                  _const_spec((Q_WIDTH, D_MODEL)), _const_spec((1, D_MODEL))],
        out_specs=tile,
        out_shape=jax.ShapeDtypeStruct(x.shape, F32),
        scratch_shapes=[pltpu.VMEM((WINDOW, KV_WIDTH), BF16),
                        pltpu.VMEM((KV_WIDTH, WINDOW), BF16)],
        compiler_params=pltpu.CompilerParams(
            dimension_semantics=("arbitrary", "arbitrary"), vmem_limit_bytes=VMEM_LIMIT),
        name="attn",
    )(sinks, x, gpre, gpost, wqkv_t, bqkv_t, cos_t, sin_t, wo, bo)


def _rope_tables_t(seq):
    inv_freq = ROPE_THETA ** (-(jnp.arange(HALF, dtype=F32) * 2.0) / HEAD_DIM)
    ang = inv_freq[:, None] * jnp.arange(seq, dtype=jnp.int32).astype(F32)[None, :]
    return jnp.cos(ang), jnp.sin(ang)


def kernel(x, norm_mix_pre, norm_mix_post, norm_ffn_pre, norm_ffn_post, attn_w_qkv, attn_b_qkv,
           attn_sinks, attn_w_o, attn_b_o, sgu_w_in, sgu_ln_g, sgu_ln_b, sgu_w_spatial,
           sgu_b_spatial, sgu_w_out, ffn_w_gate_up, ffn_w_down):
    b, s, d = x.shape
    row = lambda a: a.reshape(1, -1).astype(F32)
    cos_t, sin_t = _rope_tables_t(s)

    x = _attn(x, row(norm_mix_pre[0]), row(norm_mix_post[0]), attn_w_qkv[0].T.astype(BF16),
              attn_b_qkv[0].reshape(-1, 1).astype(F32), attn_sinks[0].astype(F32), cos_t, sin_t,
              attn_w_o[0].astype(BF16), row(attn_b_o[0]))
    x2 = x.reshape(b * s, d)
    x2 = _ffn(x2, row(norm_ffn_pre[0]), row(norm_ffn_post[0]),
              ffn_w_gate_up[0].astype(BF16), ffn_w_down[0].astype(BF16))
    bsp = jnp.repeat(sgu_b_spatial[0].T.astype(F32), D_MODEL // SGU_GROUPS, axis=1)
    x2 = _sgu(x2, row(norm_mix_pre[1]), row(norm_mix_post[1]), sgu_w_in[0].astype(BF16),
              row(sgu_ln_g[0]), row(sgu_ln_b[0]), sgu_w_spatial[0].astype(BF16), bsp,
              sgu_w_out[0].astype(BF16))
    x2 = _ffn(x2, row(norm_ffn_pre[1]), row(norm_ffn_post[1]),
              ffn_w_gate_up[1].astype(BF16), ffn_w_down[1].astype(BF16))
    return x2.reshape(b, s, d)
```

```python
import jax
import jax.numpy as jnp
from jax import lax
from jax.experimental import pallas as pl
from jax.experimental.pallas import tpu as pltpu

F32 = jnp.float32
BF16 = jnp.bfloat16

D_MODEL = 1024
HEAD_DIM = 64
HALF = HEAD_DIM // 2
N_Q_HEADS = 16
N_KV_HEADS = 4
GQA_GROUP = 4
WINDOW = 128
Q_WIDTH = N_Q_HEADS * HEAD_DIM
KV_WIDTH = N_KV_HEADS * HEAD_DIM
QKV_WIDTH = Q_WIDTH + 2 * KV_WIDTH
ROPE_THETA = 10000.0
SGU_GROUPS = 8
D_FF = 2816
EPS = 1e-6
LANES = 128

ATTN_TILE = 1024
SGU_TILE = 1024
FFN_TILE = 1024
SUB_TILE = 256
PROJ_CHUNK = 512
OUT_CHUNK = 512
LOG2E = 1.4426950408889634
VMEM_LIMIT = 56 * 1024 * 1024


def _rms(x, g):
    return x * lax.rsqrt(jnp.mean(x * x, axis=-1, keepdims=True) + EPS) * g


def _dot(a, b):
    return jnp.dot(a, b, preferred_element_type=F32)


def _skewed(n_sub, stages):
    carried = [dict() for _ in stages]
    for slot in range(n_sub + len(stages) - 1):
        for k, stage in enumerate(stages):
            t = slot - k
            if 0 <= t < n_sub:
                carried[k][t] = stage(t, carried[k - 1].pop(t) if k else None)


def _const_spec(shape):
    return pl.BlockSpec(shape, lambda *_: (0,) * len(shape), pipeline_mode=pl.Buffered(1))


def _ffn_kernel(x_ref, gpre_ref, gpost_ref, wgu_ref, wd_ref, o_ref):
    gpre = gpre_ref[...]
    gpost = gpost_ref[...]

    def rows(t):
        return pl.ds(t * SUB_TILE, SUB_TILE)

    def norm(t, _):
        return _rms(x_ref[rows(t), :], gpre).astype(BF16)

    def gate_up(t, h):
        return _dot(h, wgu_ref[...])

    def act(t, gu):
        g = gu[:, :D_FF]
        return ((g * jax.nn.sigmoid(g)) * gu[:, D_FF:]).astype(BF16)

    def down(t, a):
        return _dot(a, wd_ref[...])

    def finish(t, y):
        o_ref[rows(t), :] = x_ref[rows(t), :] + _rms(y, gpost)

    _skewed(FFN_TILE // SUB_TILE, [norm, gate_up, act, down, finish])


def _ffn(x2, gpre, gpost, wgu, wd):
    m = x2.shape[0]
    tile = pl.BlockSpec((FFN_TILE, D_MODEL), lambda i: (i, 0))
    return pl.pallas_call(
        _ffn_kernel,
        grid=(m // FFN_TILE,),
        in_specs=[tile, _const_spec((1, D_MODEL)), _const_spec((1, D_MODEL)),
                  _const_spec((D_MODEL, 2 * D_FF)), _const_spec((D_FF, D_MODEL))],
        out_specs=tile,
        out_shape=jax.ShapeDtypeStruct(x2.shape, F32),
        compiler_params=pltpu.CompilerParams(
            dimension_semantics=("arbitrary",), vmem_limit_bytes=VMEM_LIMIT),
        name="ffn",
    )(x2, gpre, gpost, wgu, wd)


def _gelu_tanh(x):
    c = 0.7978845608028654
    return x * (0.5 * (1.0 + jnp.tanh(c * (x + 0.044715 * (x * x * x)))))


def _sgu_kernel(x_ref, gpre_ref, gpost_ref, win_ref, lng_ref, lnb_ref, ws_ref, bsp_ref,
                wout_ref, o_ref):
    gpre = gpre_ref[...]
    gpost = gpost_ref[...]
    row = lax.broadcasted_iota(jnp.int32, (WINDOW, WINDOW), 0)
    col = lax.broadcasted_iota(jnp.int32, (WINDOW, WINDOW), 1)
    causal = col <= row
    ws = [jnp.where(causal, ws_ref[g], jnp.zeros((), BF16)) for g in range(SGU_GROUPS)]
    bias = bsp_ref[...]

    def rows(t):
        return pl.ds(t * SUB_TILE, SUB_TILE)

    def norm(t, _):
        return _rms(x_ref[rows(t), :], gpre).astype(BF16)

    def project(t, h):
        return _dot(h, win_ref[...])

    def gate(t, z):
        z = _gelu_tanh(z)
        u = z[:, :D_MODEL]
        v = z[:, D_MODEL:]
        mu = jnp.mean(v, axis=-1, keepdims=True)
        vc = v - mu
        var = jnp.mean(vc * vc, axis=-1, keepdims=True)
        vn = (vc * lax.rsqrt(var + EPS) * lng_ref[...] + lnb_ref[...]).astype(BF16)
        return u, vn

    def mix(t, uv):
        u, vn = uv
        chunks = []
        for c in range(SUB_TILE // WINDOW):
            r0 = c * WINDOW
            cols = [_dot(ws[g], vn[r0:r0 + WINDOW, g * LANES:(g + 1) * LANES])
                    for g in range(SGU_GROUPS)]
            chunks.append(jnp.concatenate(cols, axis=1) + bias)
        mixed = jnp.concatenate(chunks, axis=0)
        return _dot((u * mixed).astype(BF16), wout_ref[...])

    def finish(t, y):
        o_ref[rows(t), :] = x_ref[rows(t), :] + _rms(y, gpost)

    _skewed(SGU_TILE // SUB_TILE, [norm, project, gate, mix, finish])


def _sgu(x2, gpre, gpost, win, lng, lnb, ws, bsp, wout):
    m = x2.shape[0]
    tile = pl.BlockSpec((SGU_TILE, D_MODEL), lambda i: (i, 0))
    return pl.pallas_call(
        _sgu_kernel,
        grid=(m // SGU_TILE,),
        in_specs=[tile, _const_spec((1, D_MODEL)), _const_spec((1, D_MODEL)),
                  _const_spec((D_MODEL, 2 * D_MODEL)), _const_spec((1, D_MODEL)),
                  _const_spec((1, D_MODEL)), _const_spec((SGU_GROUPS, WINDOW, WINDOW)),
                  _const_spec((WINDOW, D_MODEL)), _const_spec((D_MODEL, D_MODEL))],
        out_specs=tile,
        out_shape=jax.ShapeDtypeStruct(x2.shape, F32),
        compiler_params=pltpu.CompilerParams(
            dimension_semantics=("arbitrary",), vmem_limit_bytes=VMEM_LIMIT),
        name="sgu",
    )(x2, gpre, gpost, win, lng, lnb, ws, bsp, wout)


def _rope_t(t, cos, sin):
    out = []
    for hd in range(t.shape[0] // HEAD_DIM):
        x1 = t[hd * HEAD_DIM:hd * HEAD_DIM + HALF]
        x2 = t[hd * HEAD_DIM + HALF:(hd + 1) * HEAD_DIM]
        out.append(x1 * cos - x2 * sin)
        out.append(x2 * cos + x1 * sin)
    return jnp.concatenate(out, axis=0)


def _attn_kernel(sink_ref, x_ref, gpre_ref, gpost_ref, wqkv_t_ref, bqkv_t_ref, cos_ref, sin_ref,
                 wo_ref, bo_ref, o_ref, k_prev_ref, vt_prev_ref):
    j = pl.program_id(1)

    @pl.when(j == 0)
    def _():
        k_prev_ref[...] = jnp.zeros_like(k_prev_ref)
        vt_prev_ref[...] = jnp.zeros_like(vt_prev_ref)

    gpre = gpre_ref[...]
    gpost = gpost_ref[...]
    n_sub = ATTN_TILE // SUB_TILE
    blk_per_sub = SUB_TILE // WINDOW
    n4 = GQA_GROUP * WINDOW
    key_idx = lax.broadcasted_iota(jnp.int32, (WINDOW, n4), 0)
    qry_idx = lax.broadcasted_iota(jnp.int32, (WINDOW, n4), 1) % WINDOW
    causal = key_idx <= qry_idx
    head_in_group = lax.broadcasted_iota(jnp.int32, (1, n4), 1) // WINDOW
    zero_rows = jnp.zeros((HEAD_DIM, n4), BF16)
    neg_inf = jnp.where(j == 0, -jnp.inf, 0.0).astype(F32)

    h_t, qkv_t, q_t, o_t, o_rm, y = {}, {}, {}, {}, {}, {}
    k_blk = [k_prev_ref[...]]
    vt_blk = [vt_prev_ref[...]]

    def rows(t):
        return pl.ds(t * SUB_TILE, SUB_TILE)

    def norm(t):
        h_t[t] = _rms(x_ref[0, rows(t), :], gpre).astype(BF16).T

    def project_chunks(t):
        qkv_t[t] = [None] * (QKV_WIDTH // PROJ_CHUNK)

        def chunk(c):
            r = slice(c * PROJ_CHUNK, (c + 1) * PROJ_CHUNK)
            qkv_t[t][c] = _dot(wqkv_t_ref[r, :], h_t[t]) + bqkv_t_ref[r, :]
        return [lambda c=c: chunk(c) for c in range(QKV_WIDTH // PROJ_CHUNK)]

    def rope(t):
        lanes = slice(t * SUB_TILE, (t + 1) * SUB_TILE)
        cos = cos_ref[:, lanes]
        sin = sin_ref[:, lanes]
        parts = qkv_t.pop(t)
        n_q = Q_WIDTH // PROJ_CHUNK
        q_t[t] = jnp.concatenate(
            [(_rope_t(parts[c], cos, sin) * (HEAD_DIM ** -0.5 * LOG2E)).astype(BF16) for c in range(n_q)],
            axis=0)
        kv = jnp.concatenate(parts[n_q:], axis=0)
        k_rot = _rope_t(kv[:KV_WIDTH], cos, sin)
        v_sub = kv[KV_WIDTH:].astype(BF16)
        for i in range(blk_per_sub):
            k_blk.append(k_rot[:, i * WINDOW:(i + 1) * WINDOW].T.astype(BF16))
            vt_blk.append(v_sub[:, i * WINDOW:(i + 1) * WINDOW])

    def scores(t, i, g):
        c = g // 2
        b = t * blk_per_sub + i
        keys = jnp.concatenate([k_blk[b][:, c * LANES:(c + 1) * LANES],
                                k_blk[b + 1][:, c * LANES:(c + 1) * LANES]], axis=0)
        q_row = jnp.concatenate(
            [q_t[t][(g * GQA_GROUP + hh) * HEAD_DIM:(g * GQA_GROUP + hh + 1) * HEAD_DIM,
                    i * WINDOW:(i + 1) * WINDOW] for hh in range(GQA_GROUP)], axis=1)
        rhs = jnp.concatenate([q_row, zero_rows] if g % 2 == 0 else [zero_rows, q_row], axis=0)
        return _dot(keys, rhs)

    def attend_unit(t, i, g, s_t):
        b = t * blk_per_sub + i
        s_prev = s_t[:WINDOW]
        if b == 0:
            s_prev = s_prev + neg_inf
        s_sel = jnp.where(causal, s_t[WINDOW:], s_prev)
        sink = jnp.zeros((1, n4), F32)
        for hh in range(GQA_GROUP):
            sink = jnp.where(head_in_group == hh, sink_ref[g * GQA_GROUP + hh], sink)
        sink = sink * LOG2E
        m = jnp.maximum(jnp.max(s_sel, axis=0, keepdims=True), sink)
        p_all = jnp.exp2(s_sel - m)
        denom = jnp.sum(p_all, axis=0, keepdims=True) + jnp.exp2(sink - m)
        probs = jnp.concatenate([jnp.where(causal, 0.0, p_all),
                                 jnp.where(causal, p_all, 0.0)], axis=0).astype(BF16)
        vals = jnp.concatenate([vt_blk[b][g * HEAD_DIM:(g + 1) * HEAD_DIM],
                                vt_blk[b + 1][g * HEAD_DIM:(g + 1) * HEAD_DIM]], axis=1)
        return _dot(vals, probs) * (1.0 / denom)

    def attend(t, fillers):
        units = [(i, g) for i in range(blk_per_sub) for g in range(N_KV_HEADS)]
        o_rows = [[] for _ in range(blk_per_sub)]
        s_next = scores(t, *units[0])
        for u, (i, g) in enumerate(units):
            s_t = s_next
            if u + 1 < len(units):
                s_next = scores(t, *units[u + 1])
            o_u = attend_unit(t, i, g, s_t)
            for hh in range(GQA_GROUP):
                o_rows[i].append(o_u[:, hh * WINDOW:(hh + 1) * WINDOW])
            if fillers:
                fillers.pop(0)()
        o_t[t] = jnp.concatenate([jnp.concatenate(o_rows[i], axis=0).astype(BF16)
                                  for i in range(blk_per_sub)], axis=1)

    def out_chunks(t):
        y[t] = [None] * (D_MODEL // OUT_CHUNK)

        def chunk(c):
            if c == 0:
                o_rm[t] = o_t.pop(t).T
            cols = slice(c * OUT_CHUNK, (c + 1) * OUT_CHUNK)
            y[t][c] = _dot(o_rm[t], wo_ref[:, cols]) + bo_ref[:, cols]
        return [lambda c=c: chunk(c) for c in range(D_MODEL // OUT_CHUNK)]

    def finish(t):
        o_ref[0, rows(t), :] = x_ref[0, rows(t), :] + _rms(jnp.concatenate(y.pop(t), axis=1), gpost)

    for slot in range(n_sub + 4):
        if slot < n_sub:
            norm(slot)
        fillers = []
        if 0 <= slot - 1 < n_sub:
            fillers += project_chunks(slot - 1)
        if 0 <= slot - 3 < n_sub:
            fillers += out_chunks(slot - 3)
        if 0 <= slot - 2 < n_sub:
            rope(slot - 2)
            attend(slot - 2, fillers)
        for f in fillers:
            f()
        if 0 <= slot - 4 < n_sub:
            finish(slot - 4)

    k_prev_ref[...] = k_blk[-1]
    vt_prev_ref[...] = vt_blk[-1]


def _attn(x, gpre, gpost, wqkv_t, bqkv_t, sinks, cos_t, sin_t, wo, bo):
    b, s, _ = x.shape
    tile = pl.BlockSpec((1, ATTN_TILE, D_MODEL), lambda bi, ji: (bi, ji, 0))
    rope_spec = pl.BlockSpec((HALF, ATTN_TILE), lambda bi, ji: (0, ji))
    return pl.pallas_call(
        _attn_kernel,
        grid=(b, s // ATTN_TILE),
        in_specs=[pl.BlockSpec(memory_space=pltpu.SMEM),
                  tile, _const_spec((1, D_MODEL)), _const_spec((1, D_MODEL)),
                  _const_spec((QKV_WIDTH, D_MODEL)), _const_spec((QKV_WIDTH, 1)),
                  rope_spec, rope_spec,
                  _const_spec((Q_WIDTH, D_MODEL)), _const_spec((1, D_MODEL))],
        out_specs=tile,
        out_shape=jax.ShapeDtypeStruct(x.shape, F32),
        scratch_shapes=[pltpu.VMEM((WINDOW, KV_WIDTH), BF16),
                        pltpu.VMEM((KV_WIDTH, WINDOW), BF16)],
        compiler_params=pltpu.CompilerParams(
            dimension_semantics=("arbitrary", "arbitrary"), vmem_limit_bytes=VMEM_LIMIT),
        name="attn",
    )(sinks, x, gpre, gpost, wqkv_t, bqkv_t, cos_t, sin_t, wo, bo)


def _rope_tables_t(seq):
    inv_freq = ROPE_THETA ** (-(jnp.arange(HALF, dtype=F32) * 2.0) / HEAD_DIM)
    ang = inv_freq[:, None] * jnp.arange(seq, dtype=jnp.int32).astype(F32)[None, :]
    return jnp.cos(ang), jnp.sin(ang)


def kernel(x, norm_mix_pre, norm_mix_post, norm_ffn_pre, norm_ffn_post, attn_w_qkv, attn_b_qkv,
           attn_sinks, attn_w_o, attn_b_o, sgu_w_in, sgu_ln_g, sgu_ln_b, sgu_w_spatial,
           sgu_b_spatial, sgu_w_out, ffn_w_gate_up, ffn_w_down):
    b, s, d = x.shape
    row = lambda a: a.reshape(1, -1).astype(F32)
    cos_t, sin_t = _rope_tables_t(s)

    x = _attn(x, row(norm_mix_pre[0]), row(norm_mix_post[0]), attn_w_qkv[0].T.astype(BF16),
              attn_b_qkv[0].reshape(-1, 1).astype(F32), attn_sinks[0].astype(F32), cos_t, sin_t,
              attn_w_o[0].astype(BF16), row(attn_b_o[0]))
    x2 = x.reshape(b * s, d)
    x2 = _ffn(x2, row(norm_ffn_pre[0]), row(norm_ffn_post[0]),
              ffn_w_gate_up[0].astype(BF16), ffn_w_down[0].astype(BF16))
    bsp = jnp.repeat(sgu_b_spatial[0].T.astype(F32), D_MODEL // SGU_GROUPS, axis=1)
    x2 = _sgu(x2, row(norm_mix_pre[1]), row(norm_mix_post[1]), sgu_w_in[0].astype(BF16),
              row(sgu_ln_g[0]), row(sgu_ln_b[0]), sgu_w_spatial[0].astype(BF16), bsp,
              sgu_w_out[0].astype(BF16))
    x2 = _ffn(x2, row(norm_ffn_pre[1]), row(norm_ffn_post[1]),
              ffn_w_gate_up[1].astype(BF16), ffn_w_down[1].astype(BF16))
    return x2.reshape(b, s, d)
```

```python
import jax
import jax.numpy as jnp
from jax import lax
from jax.experimental import pallas as pl
from jax.experimental.pallas import tpu as pltpu

F32 = jnp.float32
BF16 = jnp.bfloat16

D_MODEL = 1024
HEAD_DIM = 64
HALF = HEAD_DIM // 2
N_Q_HEADS = 16
N_KV_HEADS = 4
GQA_GROUP = 4
WINDOW = 128
Q_WIDTH = N_Q_HEADS * HEAD_DIM
KV_WIDTH = N_KV_HEADS * HEAD_DIM
QKV_WIDTH = Q_WIDTH + 2 * KV_WIDTH
ROPE_THETA = 10000.0
SGU_GROUPS = 8
D_FF = 2816
EPS = 1e-6
LANES = 128

ATTN_TILE = 1024
SGU_TILE = 1024
FFN_TILE = 1024
SUB_TILE = 256
PROJ_CHUNK = 256
OUT_CHUNK = 256
SCORE_LOOKAHEAD = 2
LOG2E = 1.4426950408889634
VMEM_LIMIT = 56 * 1024 * 1024


def _rms(x, g):
    return x * lax.rsqrt(jnp.mean(x * x, axis=-1, keepdims=True) + EPS) * g


def _dot(a, b):
    return jnp.dot(a, b, preferred_element_type=F32)


def _skewed(n_sub, stages):
    carried = [dict() for _ in stages]
    for slot in range(n_sub + len(stages) - 1):
        for k, stage in enumerate(stages):
            t = slot - k
            if 0 <= t < n_sub:
                carried[k][t] = stage(t, carried[k - 1].pop(t) if k else None)


def _const_spec(shape):
    return pl.BlockSpec(shape, lambda *_: (0,) * len(shape), pipeline_mode=pl.Buffered(1))


def _ffn_kernel(x_ref, gpre_ref, gpost_ref, wgu_ref, wd_ref, o_ref):
    gpre = gpre_ref[...]
    gpost = gpost_ref[...]

    def rows(t):
        return pl.ds(t * SUB_TILE, SUB_TILE)

    def norm(t, _):
        return _rms(x_ref[rows(t), :], gpre).astype(BF16)

    def gate_up(t, h):
        return _dot(h, wgu_ref[...])

    def act(t, gu):
        g = gu[:, :D_FF]
        return ((g * jax.nn.sigmoid(g)) * gu[:, D_FF:]).astype(BF16)

    def down(t, a):
        return _dot(a, wd_ref[...])

    def finish(t, y):
        o_ref[rows(t), :] = x_ref[rows(t), :] + _rms(y, gpost)

    _skewed(FFN_TILE // SUB_TILE, [norm, gate_up, act, down, finish])


def _ffn(x2, gpre, gpost, wgu, wd):
    m = x2.shape[0]
    tile = pl.BlockSpec((FFN_TILE, D_MODEL), lambda i: (i, 0))
    return pl.pallas_call(
        _ffn_kernel,
        grid=(m // FFN_TILE,),
        in_specs=[tile, _const_spec((1, D_MODEL)), _const_spec((1, D_MODEL)),
                  _const_spec((D_MODEL, 2 * D_FF)), _const_spec((D_FF, D_MODEL))],
        out_specs=tile,
        out_shape=jax.ShapeDtypeStruct(x2.shape, F32),
        compiler_params=pltpu.CompilerParams(
            dimension_semantics=("arbitrary",), vmem_limit_bytes=VMEM_LIMIT),
        name="ffn",
    )(x2, gpre, gpost, wgu, wd)


def _gelu_tanh(x):
    c = 0.7978845608028654
    return x * (0.5 * (1.0 + jnp.tanh(c * (x + 0.044715 * (x * x * x)))))


def _sgu_kernel(x_ref, gpre_ref, gpost_ref, win_ref, lng_ref, lnb_ref, ws_ref, bsp_ref,
                wout_ref, o_ref):
    gpre = gpre_ref[...]
    gpost = gpost_ref[...]
    row = lax.broadcasted_iota(jnp.int32, (WINDOW, WINDOW), 0)
    col = lax.broadcasted_iota(jnp.int32, (WINDOW, WINDOW), 1)
    causal = col <= row
    ws = [jnp.where(causal, ws_ref[g], jnp.zeros((), BF16)) for g in range(SGU_GROUPS)]
    bias = bsp_ref[...]

    def rows(t):
        return pl.ds(t * SUB_TILE, SUB_TILE)

    def norm(t, _):
        return _rms(x_ref[rows(t), :], gpre).astype(BF16)

    def project(t, h):
        return _dot(h, win_ref[...])

    def gate(t, z):
        z = _gelu_tanh(z)
        u = z[:, :D_MODEL]
        v = z[:, D_MODEL:]
        mu = jnp.mean(v, axis=-1, keepdims=True)
        vc = v - mu
        var = jnp.mean(vc * vc, axis=-1, keepdims=True)
        vn = (vc * lax.rsqrt(var + EPS) * lng_ref[...] + lnb_ref[...]).astype(BF16)
        return u, vn

    def mix(t, uv):
        u, vn = uv
        chunks = []
        for c in range(SUB_TILE // WINDOW):
            r0 = c * WINDOW
            cols = [_dot(ws[g], vn[r0:r0 + WINDOW, g * LANES:(g + 1) * LANES])
                    for g in range(SGU_GROUPS)]
            chunks.append(jnp.concatenate(cols, axis=1) + bias)
        mixed = jnp.concatenate(chunks, axis=0)
        return _dot((u * mixed).astype(BF16), wout_ref[...])

    def finish(t, y):
        o_ref[rows(t), :] = x_ref[rows(t), :] + _rms(y, gpost)

    _skewed(SGU_TILE // SUB_TILE, [norm, project, gate, mix, finish])


def _sgu(x2, gpre, gpost, win, lng, lnb, ws, bsp, wout):
    m = x2.shape[0]
    tile = pl.BlockSpec((SGU_TILE, D_MODEL), lambda i: (i, 0))
    return pl.pallas_call(
        _sgu_kernel,
        grid=(m // SGU_TILE,),
        in_specs=[tile, _const_spec((1, D_MODEL)), _const_spec((1, D_MODEL)),
                  _const_spec((D_MODEL, 2 * D_MODEL)), _const_spec((1, D_MODEL)),
                  _const_spec((1, D_MODEL)), _const_spec((SGU_GROUPS, WINDOW, WINDOW)),
                  _const_spec((WINDOW, D_MODEL)), _const_spec((D_MODEL, D_MODEL))],
        out_specs=tile,
        out_shape=jax.ShapeDtypeStruct(x2.shape, F32),
        compiler_params=pltpu.CompilerParams(
            dimension_semantics=("arbitrary",), vmem_limit_bytes=VMEM_LIMIT),
        name="sgu",
    )(x2, gpre, gpost, win, lng, lnb, ws, bsp, wout)


def _rope_t(t, cos, sin):
    out = []
    for hd in range(t.shape[0] // HEAD_DIM):
        x1 = t[hd * HEAD_DIM:hd * HEAD_DIM + HALF]
        x2 = t[hd * HEAD_DIM + HALF:(hd + 1) * HEAD_DIM]
        out.append(x1 * cos - x2 * sin)
        out.append(x2 * cos + x1 * sin)
    return jnp.concatenate(out, axis=0)


def _attn_kernel(sink_ref, x_ref, gpre_ref, gpost_ref, wqkv_t_ref, bqkv_t_ref, cos_ref, sin_ref,
                 wo_ref, bo_ref, o_ref, k_prev_ref, vt_prev_ref):
    j = pl.program_id(1)

    @pl.when(j == 0)
    def _():
        k_prev_ref[...] = jnp.zeros_like(k_prev_ref)
        vt_prev_ref[...] = jnp.zeros_like(vt_prev_ref)

    gpre = gpre_ref[...]
    gpost = gpost_ref[...]
    n_sub = ATTN_TILE // SUB_TILE
    blk_per_sub = SUB_TILE // WINDOW
    n4 = GQA_GROUP * WINDOW
    key_idx = lax.broadcasted_iota(jnp.int32, (WINDOW, n4), 0)
    qry_idx = lax.broadcasted_iota(jnp.int32, (WINDOW, n4), 1) % WINDOW
    causal = key_idx <= qry_idx
    head_in_group = lax.broadcasted_iota(jnp.int32, (1, n4), 1) // WINDOW
    zero_rows = jnp.zeros((HEAD_DIM, n4), BF16)
    neg_inf = jnp.where(j == 0, -jnp.inf, 0.0).astype(F32)

    h_t, qkv_t, q_t, o_t, o_rm, y = {}, {}, {}, {}, {}, {}
    k_blk = [k_prev_ref[...]]
    vt_blk = [vt_prev_ref[...]]

    def rows(t):
        return pl.ds(t * SUB_TILE, SUB_TILE)

    def norm(t):
        h_t[t] = _rms(x_ref[0, rows(t), :], gpre).astype(BF16).T

    def project_chunks(t):
        qkv_t[t] = [None] * (QKV_WIDTH // PROJ_CHUNK)

        def chunk(c):
            r = slice(c * PROJ_CHUNK, (c + 1) * PROJ_CHUNK)
            qkv_t[t][c] = _dot(wqkv_t_ref[r, :], h_t[t]) + bqkv_t_ref[r, :]
        return [(("proj", t), lambda c=c: chunk(c)) for c in range(QKV_WIDTH // PROJ_CHUNK)]

    def rope(t):
        lanes = slice(t * SUB_TILE, (t + 1) * SUB_TILE)
        cos = cos_ref[:, lanes]
        sin = sin_ref[:, lanes]
        parts = qkv_t.pop(t)
        n_q = Q_WIDTH // PROJ_CHUNK
        q_t[t] = jnp.concatenate(
            [(_rope_t(parts[c], cos, sin) * (HEAD_DIM ** -0.5 * LOG2E)).astype(BF16) for c in range(n_q)],
            axis=0)
        kv = jnp.concatenate(parts[n_q:], axis=0)
        k_rot = _rope_t(kv[:KV_WIDTH], cos, sin)
        v_sub = kv[KV_WIDTH:].astype(BF16)
        for i in range(blk_per_sub):
            k_blk.append(k_rot[:, i * WINDOW:(i + 1) * WINDOW].T.astype(BF16))
            vt_blk.append(v_sub[:, i * WINDOW:(i + 1) * WINDOW])

    def scores(t, i, g):
        c = g // 2
        b = t * blk_per_sub + i
        keys = jnp.concatenate([k_blk[b][:, c * LANES:(c + 1) * LANES],
                                k_blk[b + 1][:, c * LANES:(c + 1) * LANES]], axis=0)
        q_row = jnp.concatenate(
            [q_t[t][(g * GQA_GROUP + hh) * HEAD_DIM:(g * GQA_GROUP + hh + 1) * HEAD_DIM,
                    i * WINDOW:(i + 1) * WINDOW] for hh in range(GQA_GROUP)], axis=1)
        rhs = jnp.concatenate([q_row, zero_rows] if g % 2 == 0 else [zero_rows, q_row], axis=0)
        return _dot(keys, rhs)

    def attend_unit(t, i, g, s_t):
        b = t * blk_per_sub + i
        s_prev = s_t[:WINDOW]
        if b == 0:
            s_prev = s_prev + neg_inf
        s_sel = jnp.where(causal, s_t[WINDOW:], s_prev)
        sink = jnp.zeros((1, n4), F32)
        for hh in range(GQA_GROUP):
            sink = jnp.where(head_in_group == hh, sink_ref[g * GQA_GROUP + hh], sink)
        sink = sink * LOG2E
        m = jnp.maximum(jnp.max(s_sel, axis=0, keepdims=True), sink)
        p_all = jnp.exp2(s_sel - m)
        denom = jnp.sum(p_all, axis=0, keepdims=True) + jnp.exp2(sink - m)
        probs = jnp.concatenate([jnp.where(causal, 0.0, p_all),
                                 jnp.where(causal, p_all, 0.0)], axis=0).astype(BF16)
        vals = jnp.concatenate([vt_blk[b][g * HEAD_DIM:(g + 1) * HEAD_DIM],
                                vt_blk[b + 1][g * HEAD_DIM:(g + 1) * HEAD_DIM]], axis=1)
        return _dot(vals, probs) * (1.0 / denom)

    def attend(t, fillers):
        units = [(i, g) for i in range(blk_per_sub) for g in range(N_KV_HEADS)]
        o_rows = [[] for _ in range(blk_per_sub)]
        s_ahead = [scores(t, *units[u]) for u in range(SCORE_LOOKAHEAD)]
        for u, (i, g) in enumerate(units):
            if u + SCORE_LOOKAHEAD < len(units):
                s_ahead.append(scores(t, *units[u + SCORE_LOOKAHEAD]))
            if fillers:
                fillers.pop(0)[1]()
            o_u = attend_unit(t, i, g, s_ahead.pop(0))
            for hh in range(GQA_GROUP):
                o_rows[i].append(o_u[:, hh * WINDOW:(hh + 1) * WINDOW])
        o_t[t] = jnp.concatenate([jnp.concatenate(o_rows[i], axis=0).astype(BF16)
                                  for i in range(blk_per_sub)], axis=1)

    def out_chunks(t):
        y[t] = [None] * (D_MODEL // OUT_CHUNK)

        def chunk(c):
            if c == 0:
                o_rm[t] = o_t.pop(t).T
            cols = slice(c * OUT_CHUNK, (c + 1) * OUT_CHUNK)
            y[t][c] = _dot(o_rm[t], wo_ref[:, cols]) + bo_ref[:, cols]
            if c == D_MODEL // OUT_CHUNK - 1:
                finish(t)
        return [(("out", t), lambda c=c: chunk(c)) for c in range(D_MODEL // OUT_CHUNK)]

    def finish(t):
        o_ref[0, rows(t), :] = x_ref[0, rows(t), :] + _rms(jnp.concatenate(y.pop(t), axis=1), gpost)

    norm(0)
    pending = project_chunks(0)
    for t in range(n_sub):
        while pending and pending[0][0] == ("proj", t):
            pending.pop(0)[1]()
        if t + 1 < n_sub:
            norm(t + 1)
            pending = project_chunks(t + 1) + pending
        rope(t)
        attend(t, pending)
        pending += out_chunks(t)
    for _, chunk in pending:
        chunk()

    k_prev_ref[...] = k_blk[-1]
    vt_prev_ref[...] = vt_blk[-1]


def _attn(x, gpre, gpost, wqkv_t, bqkv_t, sinks, cos_t, sin_t, wo, bo):
    b, s, _ = x.shape
    tile = pl.BlockSpec((1, ATTN_TILE, D_MODEL), lambda bi, ji: (bi, ji, 0))
    rope_spec = pl.BlockSpec((HALF, ATTN_TILE), lambda bi, ji: (0, ji))
    return pl.pallas_call(
        _attn_kernel,
        grid=(b, s // ATTN_TILE),
        in_specs=[pl.BlockSpec(memory_space=pltpu.SMEM),
                  tile, _const_spec((1, D_MODEL)), _const_spec((1, D_MODEL)),
                  _const_spec((QKV_WIDTH, D_MODEL)), _const_spec((QKV_WIDTH, 1)),
                  rope_spec, rope_spec,
                  _const_spec((Q_WIDTH, D_MODEL)), _const_spec((1, D_MODEL))],
        out_specs=tile,
        out_shape=jax.ShapeDtypeStruct(x.shape, F32),
        scratch_shapes=[pltpu.VMEM((WINDOW, KV_WIDTH), BF16),
                        pltpu.VMEM((KV_WIDTH, WINDOW), BF16)],
        compiler_params=pltpu.CompilerParams(
            dimension_semantics=("arbitrary", "arbitrary"), vmem_limit_bytes=VMEM_LIMIT),
        name="attn",
    )(sinks, x, gpre, gpost, wqkv_t, bqkv_t, cos_t, sin_t, wo, bo)


def _rope_tables_t(seq):
    inv_freq = ROPE_THETA ** (-(jnp.arange(HALF, dtype=F32) * 2.0) / HEAD_DIM)
    ang = inv_freq[:, None] * jnp.arange(seq, dtype=jnp.int32).astype(F32)[None, :]
    return jnp.cos(ang), jnp.sin(ang)


def kernel(x, norm_mix_pre, norm_mix_post, norm_ffn_pre, norm_ffn_post, attn_w_qkv, attn_b_qkv,
           attn_sinks, attn_w_o, attn_b_o, sgu_w_in, sgu_ln_g, sgu_ln_b, sgu_w_spatial,
           sgu_b_spatial, sgu_w_out, ffn_w_gate_up, ffn_w_down):
    b, s, d = x.shape
    row = lambda a: a.reshape(1, -1).astype(F32)
    cos_t, sin_t = _rope_tables_t(s)

    x = _attn(x, row(norm_mix_pre[0]), row(norm_mix_post[0]), attn_w_qkv[0].T.astype(BF16),
              attn_b_qkv[0].reshape(-1, 1).astype(F32), attn_sinks[0].astype(F32), cos_t, sin_t,
              attn_w_o[0].astype(BF16), row(attn_b_o[0]))
    x2 = x.reshape(b * s, d)
    x2 = _ffn(x2, row(norm_ffn_pre[0]), row(norm_ffn_post[0]),
              ffn_w_gate_up[0].astype(BF16), ffn_w_down[0].astype(BF16))
    bsp = jnp.repeat(sgu_b_spatial[0].T.astype(F32), D_MODEL // SGU_GROUPS, axis=1)
    x2 = _sgu(x2, row(norm_mix_pre[1]), row(norm_mix_post[1]), sgu_w_in[0].astype(BF16),
              row(sgu_ln_g[0]), row(sgu_ln_b[0]), sgu_w_spatial[0].astype(BF16), bsp,
              sgu_w_out[0].astype(BF16))
    x2 = _ffn(x2, row(norm_ffn_pre[1]), row(norm_ffn_post[1]),
              ffn_w_gate_up[1].astype(BF16), ffn_w_down[1].astype(BF16))
    return x2.reshape(b, s, d)
```

```python
import jax
import jax.numpy as jnp
from jax import lax
from jax.experimental import pallas as pl
from jax.experimental.pallas import tpu as pltpu

F32 = jnp.float32
BF16 = jnp.bfloat16

D_MODEL = 1024
HEAD_DIM = 64
HALF = HEAD_DIM // 2
N_Q_HEADS = 16
N_KV_HEADS = 4
GQA_GROUP = 4
WINDOW = 128
Q_WIDTH = N_Q_HEADS * HEAD_DIM
KV_WIDTH = N_KV_HEADS * HEAD_DIM
QKV_WIDTH = Q_WIDTH + 2 * KV_WIDTH
ROPE_THETA = 10000.0
SGU_GROUPS = 8
D_FF = 2816
EPS = 1e-6
LANES = 128

ATTN_TILE = 1024
SGU_TILE = 1024
FFN_TILE = 1024
SUB_TILE = 256
PROJ_CHUNK = 256
OUT_CHUNK = 256
SCORE_LOOKAHEAD = 2
LOG2E = 1.4426950408889634
VMEM_LIMIT = 56 * 1024 * 1024


def _rms(x, g):
    return x * lax.rsqrt(jnp.mean(x * x, axis=-1, keepdims=True) + EPS) * g


def _dot(a, b):
    return jnp.dot(a, b, preferred_element_type=F32)


def _skewed(n_sub, stages):
    carried = [dict() for _ in stages]
    for slot in range(n_sub + len(stages) - 1):
        for k, stage in enumerate(stages):
            t = slot - k
            if 0 <= t < n_sub:
                carried[k][t] = stage(t, carried[k - 1].pop(t) if k else None)


def _const_spec(shape):
    return pl.BlockSpec(shape, lambda *_: (0,) * len(shape), pipeline_mode=pl.Buffered(1))


def _ffn_kernel(x_ref, gpre_ref, gpost_ref, wgu_ref, wd_ref, o_ref):
    gpre = gpre_ref[...]
    gpost = gpost_ref[...]

    def rows(t):
        return pl.ds(t * SUB_TILE, SUB_TILE)

    def norm(t, _):
        return _rms(x_ref[rows(t), :], gpre).astype(BF16)

    def gate_up(t, h):
        return _dot(h, wgu_ref[...])

    def act(t, gu):
        g = gu[:, :D_FF]
        return ((g * jax.nn.sigmoid(g)) * gu[:, D_FF:]).astype(BF16)

    def down(t, a):
        return _dot(a, wd_ref[...])

    def finish(t, y):
        o_ref[rows(t), :] = x_ref[rows(t), :] + _rms(y, gpost)

    _skewed(FFN_TILE // SUB_TILE, [norm, gate_up, act, down, finish])


def _ffn(x2, gpre, gpost, wgu, wd):
    m = x2.shape[0]
    tile = pl.BlockSpec((FFN_TILE, D_MODEL), lambda i: (i, 0))
    return pl.pallas_call(
        _ffn_kernel,
        grid=(m // FFN_TILE,),
        in_specs=[tile, _const_spec((1, D_MODEL)), _const_spec((1, D_MODEL)),
                  _const_spec((D_MODEL, 2 * D_FF)), _const_spec((D_FF, D_MODEL))],
        out_specs=tile,
        out_shape=jax.ShapeDtypeStruct(x2.shape, F32),
        compiler_params=pltpu.CompilerParams(
            dimension_semantics=("arbitrary",), vmem_limit_bytes=VMEM_LIMIT),
        name="ffn",
    )(x2, gpre, gpost, wgu, wd)


def _gelu_tanh(x):
    c = 0.7978845608028654
    t = jnp.tanh(x * (c + (c * 0.044715) * (x * x)))
    return x * (0.5 + 0.5 * t)


def _sgu_kernel(x_ref, gpre_ref, gpost_ref, win_ref, lng_ref, lnb_ref, ws_ref, bsp_ref,
                wout_ref, o_ref):
    gpre = gpre_ref[...]
    gpost = gpost_ref[...]
    row = lax.broadcasted_iota(jnp.int32, (WINDOW, WINDOW), 0)
    col = lax.broadcasted_iota(jnp.int32, (WINDOW, WINDOW), 1)
    causal = col <= row
    ws = [jnp.where(causal, ws_ref[g], jnp.zeros((), BF16)) for g in range(SGU_GROUPS)]
    bias = bsp_ref[...]

    def rows(t):
        return pl.ds(t * SUB_TILE, SUB_TILE)

    def norm(t, _):
        return _rms(x_ref[rows(t), :], gpre).astype(BF16)

    def project(t, h):
        return _dot(h, win_ref[...])

    def gate(t, z):
        z = _gelu_tanh(z)
        u = z[:, :D_MODEL]
        v = z[:, D_MODEL:]
        mu = jnp.mean(v, axis=-1, keepdims=True)
        vc = v - mu
        var = jnp.mean(vc * vc, axis=-1, keepdims=True)
        vn = (vc * lax.rsqrt(var + EPS) * lng_ref[...] + lnb_ref[...]).astype(BF16)
        return u, vn

    def mix(t, uv):
        u, vn = uv
        chunks = []
        for c in range(SUB_TILE // WINDOW):
            r0 = c * WINDOW
            cols = [_dot(ws[g], vn[r0:r0 + WINDOW, g * LANES:(g + 1) * LANES])
                    for g in range(SGU_GROUPS)]
            chunks.append(jnp.concatenate(cols, axis=1) + bias)
        return (u * jnp.concatenate(chunks, axis=0)).astype(BF16)

    def project_out(t, gated):
        return _dot(gated, wout_ref[...])

    def finish(t, y):
        o_ref[rows(t), :] = x_ref[rows(t), :] + _rms(y, gpost)

    _skewed(SGU_TILE // SUB_TILE, [norm, project, gate, mix, project_out, finish])


def _sgu(x2, gpre, gpost, win, lng, lnb, ws, bsp, wout):
    m = x2.shape[0]
    tile = pl.BlockSpec((SGU_TILE, D_MODEL), lambda i: (i, 0))
    return pl.pallas_call(
        _sgu_kernel,
        grid=(m // SGU_TILE,),
        in_specs=[tile, _const_spec((1, D_MODEL)), _const_spec((1, D_MODEL)),
                  _const_spec((D_MODEL, 2 * D_MODEL)), _const_spec((1, D_MODEL)),
                  _const_spec((1, D_MODEL)), _const_spec((SGU_GROUPS, WINDOW, WINDOW)),
                  _const_spec((WINDOW, D_MODEL)), _const_spec((D_MODEL, D_MODEL))],
        out_specs=tile,
        out_shape=jax.ShapeDtypeStruct(x2.shape, F32),
        compiler_params=pltpu.CompilerParams(
            dimension_semantics=("arbitrary",), vmem_limit_bytes=VMEM_LIMIT),
        name="sgu",
    )(x2, gpre, gpost, win, lng, lnb, ws, bsp, wout)


def _rope_t(t, cos, sin):
    out = []
    for hd in range(t.shape[0] // HEAD_DIM):
        x1 = t[hd * HEAD_DIM:hd * HEAD_DIM + HALF]
        x2 = t[hd * HEAD_DIM + HALF:(hd + 1) * HEAD_DIM]
        out.append(x1 * cos - x2 * sin)
        out.append(x2 * cos + x1 * sin)
    return jnp.concatenate(out, axis=0)


def _attn_kernel(sink_ref, x_ref, gpre_ref, gpost_ref, wqkv_t_ref, bqkv_t_ref, cos_ref, sin_ref,
                 wo_ref, bo_ref, o_ref, k_prev_ref, vt_prev_ref):
    j = pl.program_id(1)

    @pl.when(j == 0)
    def _():
        k_prev_ref[...] = jnp.zeros_like(k_prev_ref)
        vt_prev_ref[...] = jnp.zeros_like(vt_prev_ref)

    gpre = gpre_ref[...]
    gpost = gpost_ref[...]
    n_sub = ATTN_TILE // SUB_TILE
    blk_per_sub = SUB_TILE // WINDOW
    n4 = GQA_GROUP * WINDOW
    key_idx = lax.broadcasted_iota(jnp.int32, (WINDOW, n4), 0)
    qry_idx = lax.broadcasted_iota(jnp.int32, (WINDOW, n4), 1) % WINDOW
    causal = key_idx <= qry_idx
    head_in_group = lax.broadcasted_iota(jnp.int32, (1, n4), 1) // WINDOW
    zero_rows = jnp.zeros((HEAD_DIM, n4), BF16)
    neg_inf = jnp.where(j == 0, -jnp.inf, 0.0).astype(F32)

    h_t, qkv_t, q_t, o_t, o_rm, y = {}, {}, {}, {}, {}, {}
    k_blk = [k_prev_ref[...]]
    vt_blk = [vt_prev_ref[...]]

    def rows(t):
        return pl.ds(t * SUB_TILE, SUB_TILE)

    def norm(t):
        h_t[t] = _rms(x_ref[0, rows(t), :], gpre).astype(BF16).T

    def project_chunks(t):
        qkv_t[t] = [None] * (QKV_WIDTH // PROJ_CHUNK)

        def chunk(c):
            r = slice(c * PROJ_CHUNK, (c + 1) * PROJ_CHUNK)
            qkv_t[t][c] = _dot(wqkv_t_ref[r, :], h_t[t]) + bqkv_t_ref[r, :]
        return [(("proj", t), lambda c=c: chunk(c)) for c in range(QKV_WIDTH // PROJ_CHUNK)]

    def rope(t):
        lanes = slice(t * SUB_TILE, (t + 1) * SUB_TILE)
        cos = cos_ref[:, lanes]
        sin = sin_ref[:, lanes]
        parts = qkv_t.pop(t)
        n_q = Q_WIDTH // PROJ_CHUNK
        q_t[t] = jnp.concatenate(
            [(_rope_t(parts[c], cos, sin) * (HEAD_DIM ** -0.5 * LOG2E)).astype(BF16) for c in range(n_q)],
            axis=0)
        kv = jnp.concatenate(parts[n_q:], axis=0)
        k_rot = _rope_t(kv[:KV_WIDTH], cos, sin)
        v_sub = kv[KV_WIDTH:].astype(BF16)
        for i in range(blk_per_sub):
            k_blk.append(k_rot[:, i * WINDOW:(i + 1) * WINDOW].T.astype(BF16))
            vt_blk.append(v_sub[:, i * WINDOW:(i + 1) * WINDOW])

    def scores(t, i, g):
        c = g // 2
        b = t * blk_per_sub + i
        keys = jnp.concatenate([k_blk[b][:, c * LANES:(c + 1) * LANES],
                                k_blk[b + 1][:, c * LANES:(c + 1) * LANES]], axis=0)
        q_row = jnp.concatenate(
            [q_t[t][(g * GQA_GROUP + hh) * HEAD_DIM:(g * GQA_GROUP + hh + 1) * HEAD_DIM,
                    i * WINDOW:(i + 1) * WINDOW] for hh in range(GQA_GROUP)], axis=1)
        rhs = jnp.concatenate([q_row, zero_rows] if g % 2 == 0 else [zero_rows, q_row], axis=0)
        return _dot(keys, rhs)

    def attend_unit(t, i, g, s_t):
        b = t * blk_per_sub + i
        s_prev = s_t[:WINDOW]
        if b == 0:
            s_prev = s_prev + neg_inf
        s_sel = jnp.where(causal, s_t[WINDOW:], s_prev)
        sink = jnp.zeros((1, n4), F32)
        for hh in range(GQA_GROUP):
            sink = jnp.where(head_in_group == hh, sink_ref[g * GQA_GROUP + hh], sink)
        sink = sink * LOG2E
        m = jnp.maximum(jnp.max(s_sel, axis=0, keepdims=True), sink)
        p_all = jnp.exp2(s_sel - m)
        denom = jnp.sum(p_all, axis=0, keepdims=True) + jnp.exp2(sink - m)
        probs = jnp.concatenate([jnp.where(causal, 0.0, p_all),
                                 jnp.where(causal, p_all, 0.0)], axis=0).astype(BF16)
        vals = jnp.concatenate([vt_blk[b][g * HEAD_DIM:(g + 1) * HEAD_DIM],
                                vt_blk[b + 1][g * HEAD_DIM:(g + 1) * HEAD_DIM]], axis=1)
        return _dot(vals, probs) * (1.0 / denom)

    def attend(t, fillers):
        units = [(i, g) for i in range(blk_per_sub) for g in range(N_KV_HEADS)]
        o_rows = [[] for _ in range(blk_per_sub)]
        s_ahead = [scores(t, *units[u]) for u in range(SCORE_LOOKAHEAD)]
        for u, (i, g) in enumerate(units):
            if u + SCORE_LOOKAHEAD < len(units):
                s_ahead.append(scores(t, *units[u + SCORE_LOOKAHEAD]))
            if fillers:
                fillers.pop(0)[1]()
            o_u = attend_unit(t, i, g, s_ahead.pop(0))
            for hh in range(GQA_GROUP):
                o_rows[i].append(o_u[:, hh * WINDOW:(hh + 1) * WINDOW])
        o_t[t] = jnp.concatenate([jnp.concatenate(o_rows[i], axis=0).astype(BF16)
                                  for i in range(blk_per_sub)], axis=1)

    def out_chunks(t):
        y[t] = [None] * (D_MODEL // OUT_CHUNK)

        def chunk(c):
            if c == 0:
                o_rm[t] = o_t.pop(t).T
            cols = slice(c * OUT_CHUNK, (c + 1) * OUT_CHUNK)
            y[t][c] = _dot(o_rm[t], wo_ref[:, cols]) + bo_ref[:, cols]
            if c == D_MODEL // OUT_CHUNK - 1:
                finish(t)
        return [(("out", t), lambda c=c: chunk(c)) for c in range(D_MODEL // OUT_CHUNK)]

    def finish(t):
        o_ref[0, rows(t), :] = x_ref[0, rows(t), :] + _rms(jnp.concatenate(y.pop(t), axis=1), gpost)

    norm(0)
    pending = project_chunks(0)
    for t in range(n_sub):
        while pending and pending[0][0] == ("proj", t):
            pending.pop(0)[1]()
        if t + 1 < n_sub:
            norm(t + 1)
            pending = project_chunks(t + 1) + pending
        rope(t)
        attend(t, pending)
        pending += out_chunks(t)
    for _, chunk in pending:
        chunk()

    k_prev_ref[...] = k_blk[-1]
    vt_prev_ref[...] = vt_blk[-1]


def _attn(x, gpre, gpost, wqkv_t, bqkv_t, sinks, cos_t, sin_t, wo, bo):
    b, s, _ = x.shape
    tile = pl.BlockSpec((1, ATTN_TILE, D_MODEL), lambda bi, ji: (bi, ji, 0))
    rope_spec = pl.BlockSpec((HALF, ATTN_TILE), lambda bi, ji: (0, ji))
    return pl.pallas_call(
        _attn_kernel,
        grid=(b, s // ATTN_TILE),
        in_specs=[pl.BlockSpec(memory_space=pltpu.SMEM),
                  tile, _const_spec((1, D_MODEL)), _const_spec((1, D_MODEL)),
                  _const_spec((QKV_WIDTH, D_MODEL)), _const_spec((QKV_WIDTH, 1)),
                  rope_spec, rope_spec,
                  _const_spec((Q_WIDTH, D_MODEL)), _const_spec((1, D_MODEL))],
        out_specs=tile,
        out_shape=jax.ShapeDtypeStruct(x.shape, F32),
        scratch_shapes=[pltpu.VMEM((WINDOW, KV_WIDTH), BF16),
                        pltpu.VMEM((KV_WIDTH, WINDOW), BF16)],
        compiler_params=pltpu.CompilerParams(
            dimension_semantics=("arbitrary", "arbitrary"), vmem_limit_bytes=VMEM_LIMIT),
        name="attn",
    )(sinks, x, gpre, gpost, wqkv_t, bqkv_t, cos_t, sin_t, wo, bo)


def _rope_tables_t(seq):
    inv_freq = ROPE_THETA ** (-(jnp.arange(HALF, dtype=F32) * 2.0) / HEAD_DIM)
    ang = inv_freq[:, None] * jnp.arange(seq, dtype=jnp.int32).astype(F32)[None, :]
    return jnp.cos(ang), jnp.sin(ang)


def kernel(x, norm_mix_pre, norm_mix_post, norm_ffn_pre, norm_ffn_post, attn_w_qkv, attn_b_qkv,
           attn_sinks, attn_w_o, attn_b_o, sgu_w_in, sgu_ln_g, sgu_ln_b, sgu_w_spatial,
           sgu_b_spatial, sgu_w_out, ffn_w_gate_up, ffn_w_down):
    b, s, d = x.shape
    row = lambda a: a.reshape(1, -1).astype(F32)
    cos_t, sin_t = _rope_tables_t(s)

    x = _attn(x, row(norm_mix_pre[0]), row(norm_mix_post[0]), attn_w_qkv[0].T.astype(BF16),
              attn_b_qkv[0].reshape(-1, 1).astype(F32), attn_sinks[0].astype(F32), cos_t, sin_t,
              attn_w_o[0].astype(BF16), row(attn_b_o[0]))
    x2 = x.reshape(b * s, d)
    x2 = _ffn(x2, row(norm_ffn_pre[0]), row(norm_ffn_post[0]),
              ffn_w_gate_up[0].astype(BF16), ffn_w_down[0].astype(BF16))
    bsp = jnp.repeat(sgu_b_spatial[0].T.astype(F32), D_MODEL // SGU_GROUPS, axis=1)
    x2 = _sgu(x2, row(norm_mix_pre[1]), row(norm_mix_post[1]), sgu_w_in[0].astype(BF16),
              row(sgu_ln_g[0]), row(sgu_ln_b[0]), sgu_w_spatial[0].astype(BF16), bsp,
              sgu_w_out[0].astype(BF16))
    x2 = _ffn(x2, row(norm_ffn_pre[1]), row(norm_ffn_post[1]),
              ffn_w_gate_up[1].astype(BF16), ffn_w_down[1].astype(BF16))
    return x2.reshape(b, s, d)
```
